```python
import math
import jax, jax.numpy as jnp
from jax import lax
import numpy as np

D_MODEL = 1024
BATCH = 8
SEQ = 2048
DEPTH = 4
DEC_BATCH = 128
DEC_SEQ = 1
PAST_LEN = 16384
PAGE_SIZE = 128

HG_HEADS = 8
HG_DK = 128
HG_DV = D_MODEL // HG_HEADS
HG_WIDTH = HG_HEADS * HG_DK
HG_VWIDTH = HG_HEADS * HG_DV
HG_CHUNK = 32
SSM_EXPAND = 2
SSM_INNER = SSM_EXPAND * D_MODEL
SSM_HEADDIM = 64
SSM_HEADS = SSM_INNER // SSM_HEADDIM
SSM_GROUPS = 8
SSM_HPG = SSM_HEADS // SSM_GROUPS
SSM_STATE = 128
SSM_CONV = 4
SSM_CONV_CH = SSM_INNER + 2 * SSM_GROUPS * SSM_STATE
SSM_CHUNK = 64
D_FF = 4 * D_MODEL
N_ADA = 6
EPS = 1e-6
IN_SIZES = (HG_WIDTH, HG_WIDTH, HG_VWIDTH, HG_VWIDTH, SSM_INNER, SSM_CONV_CH, SSM_HEADS, D_MODEL, D_MODEL)
IN_WIDTH = sum(IN_SIZES)
IN_OFFSETS = tuple(int(v) for v in np.cumsum(IN_SIZES)[:-1])

kernel_name = "hgrn2_mamba2_gated_hybrid_step"


def _rmsnorm(x, w):
    xf = x.astype(jnp.float32)
    xf = xf * lax.rsqrt(jnp.mean(jnp.square(xf), axis=-1, keepdims=True) + EPS)
    return (xf * w.astype(jnp.float32)).astype(x.dtype)


def _chunk(a, C):
    B, T = a.shape[:2]
    pad = (-T) % C
    a = jnp.pad(a, [(0, 0), (0, pad)] + [(0, 0)] * (a.ndim - 2))
    a = a.reshape((B, (T + pad) // C, C) + a.shape[2:])
    return jnp.moveaxis(a, 1, 0)


def _unchunk(a, T):
    a = jnp.moveaxis(a, 0, 1)
    a = a.reshape((a.shape[0], -1) + a.shape[3:])
    return a[:, :T]


def _hgrn2_recurrence(q, k, v, logf, S0):
    T = q.shape[1]
    C = min(HG_CHUNK, T)
    xs = tuple(_chunk(a, C) for a in (q, k, v, logf))
    causal = jnp.tril(jnp.ones((C, C), dtype=bool))[None, :, :, None, None]

    def step(S, inp):
        qc, kc, vc, gc = inp
        b = jnp.cumsum(gc, axis=1)
        b_end = b[:, -1]
        o_inter = jnp.einsum('bihd,bhde->bihe', qc * jnp.exp(b), S)
        rel = b[:, :, None] - b[:, None, :]
        decay = jnp.exp(jnp.where(causal, rel, -jnp.inf))
        scores = jnp.einsum('bihd,bjhd,bijhd->bhij', qc, kc, decay)
        o_intra = jnp.einsum('bhij,bjhe->bihe', scores, vc)
        S = jnp.exp(b_end)[..., None] * S + jnp.einsum(
            'bjhd,bjhe->bhde', kc * jnp.exp(b_end[:, None] - b), vc)
        return S, o_inter + o_intra

    S, o = lax.scan(step, S0, xs)
    return _unchunk(o, T), S


def _ssd_recurrence(x, la, Bm, Cm, h0):
    T = x.shape[1]
    C = min(SSM_CHUNK, T)
    xs = tuple(_chunk(a, C) for a in (x, la, Bm, Cm))
    causal = jnp.tril(jnp.ones((C, C), dtype=bool))[None, :, :, None, None]

    def step(h, inp):
        xc, lc, bc, cc = inp
        cum = jnp.cumsum(lc, axis=1)
        y_inter = jnp.einsum('bign,bgkpn,bigk->bigkp', cc, h, jnp.exp(cum))
        rel = cum[:, :, None] - cum[:, None, :]
        L = jnp.exp(jnp.where(causal, rel, -jnp.inf))
        cb = jnp.einsum('bign,bjgn->bgij', cc, bc)
        y_intra = jnp.einsum('bgij,bijgk,bjgkp->bigkp', cb, L, xc)
        h = jnp.exp(cum[:, -1])[..., None, None] * h + jnp.einsum(
            'bjgn,bjgk,bjgkp->bgkpn', bc, jnp.exp(cum[:, -1:] - cum), xc)
        return h, y_inter + y_intra

    h, y = lax.scan(step, h0, xs)
    return _unchunk(y, T), h


def _causal_conv(u, prev, w, b):
    T = u.shape[1]
    full = jnp.concatenate([prev, u], axis=1)
    y = b + sum(full[:, j:j + T] * w[j] for j in range(SSM_CONV))
    return y, full[:, -(SSM_CONV - 1):]


def _layer(x, c, S_hg, h_ssm, conv_prev, p, lb, first_layer):
    f32 = jnp.float32
    Bsz, T, _ = x.shape
    mod = jnp.dot(jax.nn.silu(c), p['w_ada']) + p['b_ada']
    sh1, sc1, g1, sh2, sc2, g2 = [m[:, None, :] for m in jnp.split(mod, N_ADA, axis=-1)]
    h = _rmsnorm(x, p['norm_mix']) * (1 + sc1) + sh1
    proj = h @ p['w_in']
    q, f, i, og, z, xbc, dt, ga, gb = jnp.split(proj, IN_OFFSETS, axis=-1)

    ff = f.astype(f32)
    if first_layer:
        logf = jax.nn.log_sigmoid(ff)
        k = jax.nn.sigmoid(-ff)
    else:
        fg = lb + (1 - lb) * jax.nn.sigmoid(ff)
        logf = jnp.log(fg)
        k = 1 - fg
    qh = (jax.nn.silu(q.astype(f32)) * HG_DK ** -0.5).reshape(Bsz, T, HG_HEADS, HG_DK)
    kh = k.reshape(Bsz, T, HG_HEADS, HG_DK)
    lh = logf.reshape(Bsz, T, HG_HEADS, HG_DK)
    vh = i.astype(f32).reshape(Bsz, T, HG_HEADS, HG_DV)
    o_hg, S_new = _hgrn2_recurrence(qh, kh, vh, lh, S_hg.astype(f32))
    o_hg = _rmsnorm(o_hg, p['hg_norm']) * jax.nn.silu(og.astype(f32).reshape(Bsz, T, HG_HEADS, HG_DV))
    o_hg = o_hg.reshape(Bsz, T, HG_VWIDTH).astype(x.dtype)

    xbc_c, conv_new = _causal_conv(xbc, conv_prev.astype(xbc.dtype), p['conv_w'], p['conv_b'])
    xbc_c = jax.nn.silu(xbc_c).astype(f32)
    xs_, Bm, Cm = jnp.split(xbc_c, [SSM_INNER, SSM_INNER + SSM_GROUPS * SSM_STATE], axis=-1)
    dtv = jax.nn.softplus(dt.astype(f32) + p['dt_bias'].astype(f32))
    A = -jnp.exp(p['a_log'].astype(f32))
    la = (dtv * A).reshape(Bsz, T, SSM_GROUPS, SSM_HPG)
    xh = xs_.reshape(Bsz, T, SSM_GROUPS, SSM_HPG, SSM_HEADDIM)
    y, h_new = _ssd_recurrence(
        xh * dtv.reshape(Bsz, T, SSM_GROUPS, SSM_HPG, 1), la,
        Bm.reshape(Bsz, T, SSM_GROUPS, SSM_STATE), Cm.reshape(Bsz, T, SSM_GROUPS, SSM_STATE),
        h_ssm.astype(f32).reshape(Bsz, SSM_GROUPS, SSM_HPG, SSM_HEADDIM, SSM_STATE))
    y = y + p['d_skip'].astype(f32).reshape(SSM_GROUPS, SSM_HPG, 1) * xh
    y = y.reshape(Bsz, T, SSM_INNER) * jax.nn.silu(z.astype(f32))
    y = _rmsnorm(y.reshape(Bsz, T, SSM_GROUPS, SSM_INNER // SSM_GROUPS),
                 p['ssm_norm'].reshape(SSM_GROUPS, SSM_INNER // SSM_GROUPS))
    y = y.reshape(Bsz, T, SSM_INNER).astype(x.dtype)

    bma, bmb = jnp.split(p['b_merge'], 2, axis=-1)
    u = jax.nn.sigmoid(ga + bma) * (o_hg @ p['w_br_a']) + jax.nn.sigmoid(gb + bmb) * (y @ p['w_br_b'])
    x = x + g1 * (u @ p['w_out'])

    h2 = _rmsnorm(x, p['norm_mlp']) * (1 + sc2) + sh2
    x = x + g2 * (jnp.square(jax.nn.relu(h2 @ p['w_up'])) @ p['w_down'])

    S_out = S_new.astype(x.dtype)
    h_out = h_new.reshape(Bsz, SSM_HEADS, SSM_HEADDIM, SSM_STATE).astype(x.dtype)
    return x, S_out, h_out, conv_new.astype(x.dtype)


def setup_inputs(seed: int = 0) -> dict:
    key = jax.random.key(seed)
    ks = jax.random.split(key, 32)
    D = D_MODEL

    def nrm(k, shape, s):
        return s * jax.random.normal(k, shape, jnp.float32)

    dt0 = jnp.exp(jax.random.uniform(ks[16], (DEPTH, SSM_HEADS), jnp.float32,
                                     math.log(1e-3), math.log(1e-1)))
    return {
        'x_prompt': nrm(ks[0], (BATCH, SEQ, D), 1.0),
        'x_sample': nrm(ks[1], (DEC_BATCH, DEC_SEQ, D), 1.0),
        'state_hgrn': nrm(ks[2], (DEPTH, DEC_BATCH, HG_HEADS, HG_DK, HG_DV), 0.5),
        'state_ssm': nrm(ks[3], (DEPTH, DEC_BATCH, SSM_HEADS, SSM_HEADDIM, SSM_STATE), 0.5),
        'state_conv': nrm(ks[4], (DEPTH, DEC_BATCH, SSM_CONV - 1, SSM_CONV_CH), 1.0),
        'c_prompt': nrm(ks[5], (BATCH, D), 1.0),
        'c_sample': nrm(ks[6], (DEC_BATCH, D), 1.0),
        'w_ada': nrm(ks[7], (DEPTH, D, N_ADA * D), 0.5 * D ** -0.5),
        'b_ada': nrm(ks[8], (DEPTH, N_ADA * D), 0.01),
        'norm_mix': 1.0 + nrm(ks[9], (DEPTH, D), 0.05),
        'w_in': nrm(ks[10], (DEPTH, D, IN_WIDTH), D ** -0.5),
        'b_merge': nrm(ks[11], (DEPTH, 2 * D), 0.01),
        'lower_bounds': nrm(ks[12], (DEPTH, HG_WIDTH), 0.1),
        'hg_norm': 1.0 + nrm(ks[13], (DEPTH, HG_DV), 0.05),
        'conv_w': nrm(ks[14], (DEPTH, SSM_CONV, SSM_CONV_CH), SSM_CONV ** -0.5),
        'conv_b': nrm(ks[15], (DEPTH, SSM_CONV_CH), 0.01),
        'dt_bias': dt0 + jnp.log(-jnp.expm1(-dt0)),
        'a_log': jnp.log(jax.random.uniform(ks[17], (DEPTH, SSM_HEADS), jnp.float32, 1.0, 16.0)),
        'd_skip': 1.0 + nrm(ks[18], (DEPTH, SSM_HEADS), 0.05),
        'ssm_norm': 1.0 + nrm(ks[19], (DEPTH, SSM_INNER), 0.05),
        'w_br_a': nrm(ks[20], (DEPTH, HG_VWIDTH, D), HG_VWIDTH ** -0.5),
        'w_br_b': nrm(ks[21], (DEPTH, SSM_INNER, D), SSM_INNER ** -0.5),
        'w_out': nrm(ks[22], (DEPTH, D, D), D ** -0.5),
        'norm_mlp': 1.0 + nrm(ks[23], (DEPTH, D), 0.05),
        'w_up': nrm(ks[24], (DEPTH, D, D_FF), D ** -0.5),
        'w_down': nrm(ks[25], (DEPTH, D_FF, D), D_FF ** -0.5),
        'norm_final': 1.0 + nrm(ks[26], (D,), 0.05),
    }


def reference(x_prompt, x_sample, state_hgrn, state_ssm, state_conv, c_prompt, c_sample,
              w_ada, b_ada, norm_mix, w_in, b_merge, lower_bounds, hg_norm, conv_w, conv_b,
              dt_bias, a_log, d_skip, ssm_norm, w_br_a, w_br_b, w_out, norm_mlp, w_up, w_down,
              norm_final):
    lbs = jnp.cumsum(jax.nn.softmax(lower_bounds.astype(jnp.float32), axis=0), axis=0)
    lbs = lbs - lbs[0:1]
    dt = x_prompt.dtype
    xp, xs = x_prompt, x_sample
    hg_p, ssm_p, cv_p, hg_s, ssm_s, cv_s = [], [], [], [], [], []
    for l in range(DEPTH):
        p = {'w_ada': w_ada[l], 'b_ada': b_ada[l], 'norm_mix': norm_mix[l], 'w_in': w_in[l],
             'b_merge': b_merge[l], 'hg_norm': hg_norm[l], 'conv_w': conv_w[l], 'conv_b': conv_b[l],
             'dt_bias': dt_bias[l], 'a_log': a_log[l], 'd_skip': d_skip[l], 'ssm_norm': ssm_norm[l],
             'w_br_a': w_br_a[l], 'w_br_b': w_br_b[l], 'w_out': w_out[l], 'norm_mlp': norm_mlp[l],
             'w_up': w_up[l], 'w_down': w_down[l]}
        first = l == 0
        xp, s1, s2, s3 = _layer(
            xp, c_prompt,
            jnp.zeros((xp.shape[0], HG_HEADS, HG_DK, HG_DV), dt),
            jnp.zeros((xp.shape[0], SSM_HEADS, SSM_HEADDIM, SSM_STATE), dt),
            jnp.zeros((xp.shape[0], SSM_CONV - 1, SSM_CONV_CH), dt),
            p, lbs[l], first)
        hg_p.append(s1); ssm_p.append(s2); cv_p.append(s3)
        xs, t1, t2, t3 = _layer(xs, c_sample, state_hgrn[l], state_ssm[l], state_conv[l],
                                p, lbs[l], first)
        hg_s.append(t1); ssm_s.append(t2); cv_s.append(t3)
    y_prompt = _rmsnorm(xp, norm_final)
    y_sample = _rmsnorm(xs, norm_final)
    return (y_prompt, y_sample, jnp.stack(hg_p), jnp.stack(ssm_p), jnp.stack(cv_p),
            jnp.stack(hg_s), jnp.stack(ssm_s), jnp.stack(cv_s))
```

```python
import functools

import jax
import jax.numpy as jnp
from jax import lax
from jax.experimental import pallas as pl
from jax.experimental.pallas import tpu as pltpu

F32 = jnp.float32
BF16 = jnp.bfloat16

D_MODEL = 1024
BATCH = 8
SEQ = 2048
DEPTH = 4
DEC_BATCH = 128
HG_HEADS = 8
HG_DK = 128
HG_DV = 128
SSM_INNER = 2048
SSM_HEADDIM = 64
SSM_HEADS = 32
SSM_GROUPS = 8
SSM_HPG = 4
SSM_STATE = 128
SSM_CONV = 4
SSM_CONV_CH = 4096
GROUP_W = SSM_HPG * SSM_HEADDIM
D_FF = 4096
N_ADA = 6
EPS = 1e-6
DT_PAD = 128
COL_Q, COL_F, COL_I, COL_OG = 0, 1024, 2048, 3072
COL_Z, COL_XBC, COL_GA, COL_GB = 4096, 6144, 10240, 11264
PROJ_W = 12288
ADA_SH1, ADA_SC1, ADA_G1, ADA_SH2, ADA_SC2, ADA_G2 = range(6)

LANE = 128
HG_CHUNK = 32
TILE = 128
VMEM_LIMIT = 56 * 1024 * 1024
EXP_CLAMP = 80.0


def _cparams(*sem):
    return pltpu.CompilerParams(dimension_semantics=sem, vmem_limit_bytes=VMEM_LIMIT)


def _sigmoid(x):
    return 1.0 / (1.0 + jnp.exp(-x))


def _silu(x):
    return x * _sigmoid(x)


def _softplus(x):
    return jnp.maximum(x, 0.0) + jnp.log1p(jnp.exp(-jnp.abs(x)))


def _dot(a, b):
    return jnp.dot(a, b, preferred_element_type=F32)


def _dot_nt(a, b):
    return lax.dot_general(a, b, (((1,), (1,)), ((), ())), preferred_element_type=F32)


def _split3(a):
    hi = a.astype(BF16)
    r = a - hi.astype(F32)
    mid = r.astype(BF16)
    lo = (r - mid.astype(F32)).astype(BF16)
    return hi, mid, lo


def _sel_dot(sel, a):
    hi, mid, lo = _split3(a)
    return _dot(sel, hi) + _dot(sel, mid) + _dot(sel, lo)


def _dot_sel(a, sel):
    hi, mid, lo = _split3(a)
    return _dot(hi, sel) + _dot(mid, sel) + _dot(lo, sel)


def _rms(x, w):
    ms = jnp.mean(x * x, axis=-1, keepdims=True)
    return x * lax.rsqrt(ms + EPS) * w


def _hgrn_gates(f, lb, first):
    e = jnp.exp(-jnp.abs(f))
    if first:
        g = jnp.minimum(f, 0.0) - jnp.log1p(e)
        k = jnp.where(f >= 0, e, 1.0) / (1.0 + e)
    else:
        s = jnp.where(f >= 0, 1.0, e) / (1.0 + e)
        fg = lb + (1.0 - lb) * s
        g = jnp.log(fg)
        k = 1.0 - fg
    return g, k


def _lbs_kernel(lb_ref, o_ref):
    x = lb_ref[...]
    m = jnp.max(x, axis=0, keepdims=True)
    e = jnp.exp(x - m)
    p = e / jnp.sum(e, axis=0, keepdims=True)
    acc = jnp.zeros_like(p[0:1])
    rows = [acc]
    for l in range(1, DEPTH):
        acc = acc + p[l:l + 1]
        rows.append(acc)
    o_ref[...] = jnp.concatenate(rows, axis=0)


def _lower_bounds(lower_bounds):
    return pl.pallas_call(
        _lbs_kernel, out_shape=jax.ShapeDtypeStruct(lower_bounds.shape, F32), name="lbs",
    )(lower_bounds)


def _mod_kernel(c_ref, w_ref, b_ref, o_ref):
    a = _silu(c_ref[...]).astype(BF16)
    o_ref[0] = _dot(a, w_ref[0].astype(BF16)) + b_ref[0]


def _modulation(c_all, w_ada, b_ada):
    n = c_all.shape[0]
    tn = 1024
    return pl.pallas_call(
        _mod_kernel,
        grid=(DEPTH, N_ADA * D_MODEL // tn),
        in_specs=[
            pl.BlockSpec((n, D_MODEL), lambda l, j: (0, 0)),
            pl.BlockSpec((1, D_MODEL, tn), lambda l, j: (l, 0, j)),
            pl.BlockSpec((1, 1, tn), lambda l, j: (l, 0, j)),
        ],
        out_specs=pl.BlockSpec((1, n, tn), lambda l, j: (l, 0, j)),
        out_shape=jax.ShapeDtypeStruct((DEPTH, n, N_ADA * D_MODEL), F32),
        compiler_params=_cparams("arbitrary", "arbitrary"),
        name="modulation",
    )(c_all, w_ada, b_ada.reshape(DEPTH, 1, N_ADA * D_MODEL))


def _mod_spec(mod, tm, rows_per_group, piece):
    if mod.shape[1] == 1:
        tiles = rows_per_group // tm
        return pl.BlockSpec((1, 1, D_MODEL), lambda i, *_: (i // tiles, 0, piece))
    return pl.BlockSpec((1, tm, D_MODEL), lambda i, *_: (0, i, piece))


def _inproj_kernel(x_ref, sc_ref, sh_ref, nw_ref, w_ref, wdt_ref, o_ref, odt_ref, h_scr):
    @pl.when(pl.program_id(1) == 0)
    def _():
        h = _rms(x_ref[...], nw_ref[0]) * (1.0 + sc_ref[0]) + sh_ref[0]
        hb = h.astype(BF16)
        h_scr[...] = hb
        odt_ref[...] = _dot(hb, wdt_ref[0])

    o_ref[...] = _dot(h_scr[...], w_ref[0])


def _inproj(x, mod, norm_w, w_main, w_dt, l, tm, rows_per_group):
    rows = x.shape[0]
    tn = 1024
    return pl.pallas_call(
        _inproj_kernel,
        grid=(rows // tm, PROJ_W // tn),
        in_specs=[
            pl.BlockSpec((tm, D_MODEL), lambda i, j: (i, 0)),
            _mod_spec(mod, tm, rows_per_group, ADA_SC1),
            _mod_spec(mod, tm, rows_per_group, ADA_SH1),
            pl.BlockSpec((1, 1, D_MODEL), lambda i, j: (l, 0, 0)),
            pl.BlockSpec((1, D_MODEL, tn), lambda i, j: (l, 0, j)),
            pl.BlockSpec((1, D_MODEL, DT_PAD), lambda i, j: (l, 0, 0)),
        ],
        out_specs=[
            pl.BlockSpec((tm, tn), lambda i, j: (i, j)),
            pl.BlockSpec((tm, DT_PAD), lambda i, j: (i, 0)),
        ],
        out_shape=[
            jax.ShapeDtypeStruct((rows, PROJ_W), F32),
            jax.ShapeDtypeStruct((rows, DT_PAD), F32),
        ],
        scratch_shapes=[pltpu.VMEM((tm, D_MODEL), BF16)],
        compiler_params=_cparams("arbitrary", "arbitrary"),
        name="inproj",
    )(x, mod, mod, norm_w, w_main, w_dt)


def _hgrn_prompt_kernel(q_ref, f_ref, v_ref, og_ref, lb_ref, nw_ref, o_ref, s_ref, st_scr,
                        *, first, seq):
    st_scr[...] = jnp.zeros_like(st_scr)
    row = lax.broadcasted_iota(jnp.int32, (TILE, TILE), 0)
    col = lax.broadcasted_iota(jnp.int32, (TILE, TILE), 1)
    same_chunk_causal = (row // HG_CHUNK == col // HG_CHUNK) & (row >= col)
    tri = jnp.where(same_chunk_causal, 1.0, 0.0).astype(BF16)
    lb = lb_ref[0]
    nw = nw_ref[0]
    n_chunks = TILE // HG_CHUNK

    def tile_body(t, carry):
        r0 = pl.multiple_of(t * TILE, TILE)
        rows = pl.ds(r0, TILE)
        g, k = _hgrn_gates(f_ref[rows, :], lb, first)
        b = _sel_dot(tri, g)
        q = _silu(q_ref[rows, :]) * (HG_DK ** -0.5)
        v = v_ref[rows, :]
        ends = [b[(c + 1) * HG_CHUNK - 1:(c + 1) * HG_CHUNK, :] for c in range(n_chunks)]
        b_end = jnp.concatenate(
            [jnp.broadcast_to(e, (HG_CHUNK, HG_DK)) for e in ends], axis=0)
        qt = (q * jnp.exp(b)).astype(BF16)
        kt = (k * jnp.exp(jnp.minimum(-b, EXP_CLAMP))).astype(BF16)
        kd = k * jnp.exp(b_end - b)
        vb = v.astype(BF16)
        scores = jnp.where(same_chunk_causal, _dot_nt(qt, kt), 0.0)
        o = _dot(scores.astype(BF16), vb)
        vt = v.T.astype(BF16)
        st = st_scr[...]
        inter = []
        for c in range(n_chunks):
            lo, hi = c * HG_CHUNK, (c + 1) * HG_CHUNK
            inter.append(_dot_nt(qt[lo:hi], st.astype(BF16)))
            kd_c = jnp.where((row >= lo) & (row < hi), kd, 0.0).astype(BF16)
            st = st * jnp.exp(ends[c]) + _dot(vt, kd_c)
        st_scr[...] = st
        o = o + jnp.concatenate(inter, axis=0)
        o_ref[rows, :] = (_rms(o, nw) * _silu(og_ref[rows, :])).astype(o_ref.dtype)
        return carry

    lax.fori_loop(0, seq // TILE, tile_body, 0)
    s_ref[0, 0] = st_scr[...].T


def _hgrn_prompt(proj, lbs, hg_norm, l):
    seq = SEQ
    col = lambda c0: (lambda b, h: (b, c0 // LANE + h))
    blk = lambda c0: pl.BlockSpec((seq, LANE), col(c0))
    return pl.pallas_call(
        functools.partial(_hgrn_prompt_kernel, first=(l == 0), seq=seq),
        grid=(BATCH, HG_HEADS),
        in_specs=[
            blk(COL_Q), blk(COL_F), blk(COL_I), blk(COL_OG),
            pl.BlockSpec((1, 1, HG_DK), lambda b, h: (l, 0, h)),
            pl.BlockSpec((1, 1, HG_DV), lambda b, h: (l, 0, 0)),
        ],
        out_specs=[
            pl.BlockSpec((seq, HG_DV), lambda b, h: (b, h)),
            pl.BlockSpec((1, 1, HG_DK, HG_DV), lambda b, h: (b, h, 0, 0)),
        ],
        out_shape=[
            jax.ShapeDtypeStruct((BATCH * seq, HG_HEADS * HG_DV), BF16),
            jax.ShapeDtypeStruct((BATCH, HG_HEADS, HG_DK, HG_DV), F32),
        ],
        scratch_shapes=[pltpu.VMEM((HG_DV, HG_DK), F32)],
        compiler_params=_cparams("arbitrary", "arbitrary"),
        name="hgrn_prompt",
    )(proj, proj, proj, proj, lbs, hg_norm)


def _conv_silu_tile(ref, w_ref, b_ref, r0, t):
    cur = ref[pl.ds(r0, TILE), :]
    prev = ref[pl.ds(pl.multiple_of(jnp.maximum(r0 - 8, 0), 8), 8), :]
    prev = jnp.where(t > 0, prev, 0.0)
    ext = jnp.concatenate([prev, cur], axis=0)
    acc = b_ref[...] + w_ref[SSM_CONV - 1:SSM_CONV, :] * cur
    for j in range(SSM_CONV - 1):
        acc = acc + w_ref[j:j + 1, :] * ext[5 + j:5 + j + TILE, :]
    return _silu(acc)


def _ssd_prompt_kernel(x_ref, bm_ref, cm_ref, z_ref, dt_ref,
                       wx_ref, wb_ref, wc_ref, cbx_ref, cbb_ref, cbc_ref,
                       dtb_ref, alog_ref, dsk_ref, nw_ref, exp_ref,
                       y_ref, h_ref, ht_scr, *, seq):
    ht_scr[...] = jnp.zeros_like(ht_scr)
    row = lax.broadcasted_iota(jnp.int32, (TILE, TILE), 0)
    col = lax.broadcasted_iota(jnp.int32, (TILE, TILE), 1)
    causal = row >= col
    tri = jnp.where(causal, 1.0, 0.0).astype(BF16)
    head_of_lane = lax.broadcasted_iota(jnp.int32, (TILE, GROUP_W), 1) // SSM_HEADDIM
    a_neg = -jnp.exp(alog_ref[...])
    expand = exp_ref[0]

    def tile_body(t, carry):
        r0 = pl.multiple_of(t * TILE, TILE)
        rows = pl.ds(r0, TILE)
        xc = _conv_silu_tile(x_ref, wx_ref, cbx_ref, r0, t)
        bc = _conv_silu_tile(bm_ref, wb_ref, cbb_ref, r0, t)
        cc = _conv_silu_tile(cm_ref, wc_ref, cbc_ref, r0, t)
        dtv = _softplus(_dot_sel(dt_ref[rows, :], expand) + dtb_ref[...])
        la = dtv * a_neg
        cum = _sel_dot(tri, la)
        cum_end = cum[TILE - 1:TILE, :]
        cum_t = cum.T
        xdt = xc * dtv
        ccb = cc.astype(BF16)
        cb = _dot_nt(ccb, bc.astype(BF16))
        y = _dot(ccb, ht_scr[...].astype(BF16)) * jnp.exp(cum)
        for k in range(SSM_HPG):
            c_col = cum[:, k * SSM_HEADDIM:k * SSM_HEADDIM + 1]
            c_row = cum_t[k * SSM_HEADDIM:k * SSM_HEADDIM + 1, :]
            decay = jnp.exp(jnp.minimum(c_col - c_row, 0.0))
            m_k = jnp.where(causal, cb * decay, 0.0).astype(BF16)
            x_k = jnp.where(head_of_lane == k, xdt, 0.0).astype(BF16)
            y = y + _dot(m_k, x_k)
        xs = (xdt * jnp.exp(cum_end - cum)).astype(BF16)
        ht_scr[...] = ht_scr[...] * jnp.exp(cum_end) + _dot(bc.T.astype(BF16), xs)
        y = (y + dsk_ref[...] * xc) * _silu(z_ref[rows, :])
        y_ref[rows, :] = _rms(y, nw_ref[...]).astype(y_ref.dtype)
        return carry

    lax.fori_loop(0, seq // TILE, tile_body, 0)
    h_ref[0] = ht_scr[...].T.reshape(SSM_HPG, SSM_HEADDIM, SSM_STATE)


def _ssd_prompt(proj, dt, p, l):
    seq = SEQ
    xbc = COL_XBC
    gspec = lambda width, c0: pl.BlockSpec((seq, width), lambda b, g: (b, c0 // width + g))
    wspec = lambda rows, width, c0: pl.BlockSpec(
        (1, rows, width), lambda b, g: (l, 0, c0 // width + g))
    pspec = pl.BlockSpec((1, 1, GROUP_W), lambda b, g: (l, 0, g))
    args = (proj, proj, proj, proj, dt,
            p['conv_w'], p['conv_w'], p['conv_w'], p['conv_b'], p['conv_b'], p['conv_b'],
            p['dt_bias_x'], p['a_log_x'], p['d_skip_x'], p['ssm_norm'], p['expand_g'])
    in_specs = [
        gspec(GROUP_W, xbc), gspec(SSM_STATE, xbc + SSM_INNER),
        gspec(SSM_STATE, xbc + SSM_INNER + SSM_GROUPS * SSM_STATE), gspec(GROUP_W, COL_Z),
        pl.BlockSpec((seq, DT_PAD), lambda b, g: (b, 0)),
        wspec(SSM_CONV, GROUP_W, 0), wspec(SSM_CONV, SSM_STATE, SSM_INNER),
        wspec(SSM_CONV, SSM_STATE, SSM_INNER + SSM_GROUPS * SSM_STATE),
        wspec(1, GROUP_W, 0), wspec(1, SSM_STATE, SSM_INNER),
        wspec(1, SSM_STATE, SSM_INNER + SSM_GROUPS * SSM_STATE),
        pspec, pspec, pspec, pspec,
        pl.BlockSpec((1, DT_PAD, GROUP_W), lambda b, g: (g, 0, 0)),
    ]

    def body(*refs):
        ins = list(refs[:16])
        for i in range(5, 15):
            ins[i] = ins[i].at[0]
        _ssd_prompt_kernel(*ins, *refs[16:], seq=seq)

    return pl.pallas_call(
        body,
        grid=(BATCH, SSM_GROUPS),
        in_specs=in_specs,
        out_specs=[
            pl.BlockSpec((seq, GROUP_W), lambda b, g: (b, g)),
            pl.BlockSpec((1, SSM_HPG, SSM_HEADDIM, SSM_STATE), lambda b, g: (b, g, 0, 0)),
        ],
        out_shape=[
            jax.ShapeDtypeStruct((BATCH * seq, SSM_INNER), BF16),
            jax.ShapeDtypeStruct((BATCH, SSM_HEADS, SSM_HEADDIM, SSM_STATE), F32),
        ],
        scratch_shapes=[pltpu.VMEM((SSM_STATE, GROUP_W), F32)],
        compiler_params=_cparams("arbitrary", "arbitrary"),
        name="ssd_prompt",
    )(*args)


HG_STEP_NB = 8


SUBLANES = 8


def _column(mat_t, lane_mask):
    return jnp.sum(jnp.where(lane_mask, mat_t, 0.0), axis=1, keepdims=True)


def _row_group(ref, bidx, cols):
    start = pl.multiple_of((bidx // SUBLANES) * SUBLANES, SUBLANES)
    idx = (pl.ds(start, SUBLANES), cols)
    blk = ref[idx]
    is_row = lax.broadcasted_iota(jnp.int32, blk.shape, 0) == bidx % SUBLANES
    return idx, blk, is_row


def _get_row(ref, bidx, cols):
    _, blk, is_row = _row_group(ref, bidx, cols)
    return jnp.sum(jnp.where(is_row, blk, 0.0), axis=0, keepdims=True)


def _set_row(ref, bidx, cols, row):
    idx, blk, is_row = _row_group(ref, bidx, cols)
    ref[idx] = jnp.where(is_row, row, blk)


def _hgrn_step_kernel(q_ref, f_ref, v_ref, og_ref, lb_ref, nw_ref, s_ref, o_ref, so_ref,
                      dec_t, k_t, q_t, *, first):
    i = pl.program_id(0)

    @pl.when(i == 0)
    def _():
        g, k = _hgrn_gates(f_ref[...], lb_ref[0], first)
        dec = jnp.exp(g)
        q = _silu(q_ref[...]) * (HG_DK ** -0.5)
        o_ref[...] = jnp.zeros_like(o_ref)
        for h in range(HG_HEADS):
            hs = slice(h * HG_DK, (h + 1) * HG_DK)
            dec_t[hs, :] = dec[:, hs].T
            k_t[hs, :] = k[:, hs].T
            q_t[hs, :] = q[:, hs].T

    lane = lax.broadcasted_iota(jnp.int32, (HG_DK, DEC_BATCH), 1)

    def body(j, carry):
        bidx = i * HG_STEP_NB + j
        m = lane == bidx
        for h in range(HG_HEADS):
            hs = slice(h * HG_DK, (h + 1) * HG_DK)
            d_col = _column(dec_t[hs, :], m)
            k_col = _column(k_t[hs, :], m)
            q_col = _column(q_t[hs, :], m)
            v_row = _get_row(v_ref, bidx, hs)
            s_new = d_col * s_ref[0, j, h] + k_col * v_row
            so_ref[0, j, h] = s_new
            o_row = jnp.sum(q_col * s_new, axis=0, keepdims=True)
            _set_row(o_ref, bidx, hs, _rms(o_row, nw_ref[0]) * _silu(_get_row(og_ref, bidx, hs)))
        return carry

    lax.fori_loop(0, HG_STEP_NB, body, 0)


def _hgrn_step(proj, state, lbs, hg_norm, out_state, l):
    nb = HG_STEP_NB
    blk = lambda c0: pl.BlockSpec((DEC_BATCH, D_MODEL), lambda i: (0, c0 // D_MODEL))
    sspec = pl.BlockSpec((1, nb, HG_HEADS, HG_DK, HG_DV), lambda i: (l, i, 0, 0, 0))
    in_specs = [
        blk(COL_Q), blk(COL_F), blk(COL_I), blk(COL_OG),
        pl.BlockSpec((1, 1, D_MODEL), lambda i: (l, 0, 0)),
        pl.BlockSpec((1, 1, HG_DV), lambda i: (l, 0, 0)),
        sspec,
    ]
    args = [proj, proj, proj, proj, lbs, hg_norm, state]
    aliases = {}
    if out_state is not None:
        in_specs.append(pl.BlockSpec(memory_space=pl.ANY))
        args.append(out_state)
        aliases = {len(args) - 1: 1}

    def body(*refs):
        ins, rest = refs[:7], refs[len(args):]
        _hgrn_step_kernel(*ins, *rest, first=(l == 0))

    return pl.pallas_call(
        body,
        grid=(DEC_BATCH // nb,),
        in_specs=in_specs,
        out_specs=[pl.BlockSpec((DEC_BATCH, D_MODEL), lambda i: (0, 0)), sspec],
        out_shape=[
            jax.ShapeDtypeStruct((DEC_BATCH, D_MODEL), F32),
            jax.ShapeDtypeStruct(state.shape, F32),
        ],
        scratch_shapes=[pltpu.VMEM((D_MODEL, DEC_BATCH), F32)] * 3,
        input_output_aliases=aliases,
        compiler_params=_cparams("arbitrary"),
        name="hgrn_step",
    )(*args)


def _ssm_prep_kernel(xs_ref, bm_ref, cm_ref, dt_ref, cst_ref, w_ref, cb_ref,
                     dtb_ref, alog_ref, exp_ref,
                     xc_ref, bc_ref, cc_ref, dect_ref, xdtt_ref, cnew_ref):
    ch = SSM_CONV_CH
    parts = ((xs_ref, 0, SSM_INNER, xc_ref),
             (bm_ref, SSM_INNER, SSM_INNER + 1024, bc_ref),
             (cm_ref, SSM_INNER + 1024, ch, cc_ref))
    for u_ref, c0, c1, out_ref in parts:
        u = u_ref[...]
        acc = cb_ref[:, c0:c1] + w_ref[SSM_CONV - 1:SSM_CONV, c0:c1] * u
        for j in range(SSM_CONV - 1):
            acc = acc + w_ref[j:j + 1, c0:c1] * cst_ref[:, j * ch + c0:j * ch + c1]
        out_ref[...] = _silu(acc)
        cnew_ref[:, (SSM_CONV - 2) * ch + c0:(SSM_CONV - 2) * ch + c1] = u
    cnew_ref[:, 0:(SSM_CONV - 2) * ch] = cst_ref[:, ch:(SSM_CONV - 1) * ch]
    dtv = _softplus(_dot_sel(dt_ref[...], exp_ref[...]) + dtb_ref[...])
    dec = jnp.exp(dtv * -jnp.exp(alog_ref[...]))
    xdt = xc_ref[...] * dtv
    for m in range(SSM_INNER // LANE):
        ms = slice(m * LANE, (m + 1) * LANE)
        dect_ref[ms, :] = dec[:, ms].T
        xdtt_ref[ms, :] = xdt[:, ms].T


def _ssm_prep(proj, dt, conv_state, p, l):
    n = DEC_BATCH
    full = lambda shape: pl.BlockSpec(shape, lambda i: tuple(0 for _ in shape))
    lspec = lambda rows, width: pl.BlockSpec((1, rows, width), lambda i: (l, 0, 0))

    def body(xs, bm, cm, dtr, cst, w, cb, dtb, alog, ex, *outs):
        _ssm_prep_kernel(xs, bm, cm, dtr, cst.at[0], w.at[0], cb.at[0], dtb.at[0], alog.at[0],
                         ex, *outs)

    return pl.pallas_call(
        body,
        grid=(1,),
        in_specs=[
            pl.BlockSpec((n, SSM_INNER), lambda i: (0, COL_XBC // SSM_INNER)),
            pl.BlockSpec((n, 1024), lambda i: (0, (COL_XBC + SSM_INNER) // 1024)),
            pl.BlockSpec((n, 1024), lambda i: (0, (COL_XBC + SSM_INNER) // 1024 + 1)),
            full((n, DT_PAD)),
            pl.BlockSpec((1, n, (SSM_CONV - 1) * SSM_CONV_CH), lambda i: (l, 0, 0)),
            lspec(SSM_CONV, SSM_CONV_CH), lspec(1, SSM_CONV_CH),
            lspec(1, SSM_INNER), lspec(1, SSM_INNER),
            full((DT_PAD, SSM_INNER)),
        ],
        out_specs=[
            full((n, SSM_INNER)), full((n, 1024)), full((n, 1024)),
            full((SSM_INNER, n)), full((SSM_INNER, n)),
            full((n, (SSM_CONV - 1) * SSM_CONV_CH)),
        ],
        out_shape=[
            jax.ShapeDtypeStruct((n, SSM_INNER), F32),
            jax.ShapeDtypeStruct((n, 1024), F32),
            jax.ShapeDtypeStruct((n, 1024), F32),
            jax.ShapeDtypeStruct((SSM_INNER, n), F32),
            jax.ShapeDtypeStruct((SSM_INNER, n), F32),
            jax.ShapeDtypeStruct((n, (SSM_CONV - 1) * SSM_CONV_CH), F32),
        ],
        compiler_params=_cparams("arbitrary"),
        name="ssm_prep",
    )(proj, proj, proj, dt, conv_state, p['conv_w'], p['conv_b'],
      p['dt_bias_x'], p['a_log_x'], p['expand_all'])


SSM_STEP_NB = 4


def _ssm_step_kernel(dect_ref, xdtt_ref, bc_ref, cc_ref, s_ref, yt_ref, so_ref):
    i = pl.program_id(0)
    lane = lax.broadcasted_iota(jnp.int32, (GROUP_W, DEC_BATCH), 1)

    @pl.when(i == 0)
    def _():
        yt_ref[...] = jnp.zeros_like(yt_ref)

    def body(j, carry):
        bidx = i * SSM_STEP_NB + j
        m = lane == bidx
        for g in range(SSM_GROUPS):
            rs = slice(g * GROUP_W, (g + 1) * GROUP_W)
            hs = slice(g * SSM_HPG, (g + 1) * SSM_HPG)
            ns = slice(g * SSM_STATE, (g + 1) * SSM_STATE)
            d_col = _column(dect_ref[rs, :], m)
            x_col = _column(xdtt_ref[rs, :], m)
            b_row = _get_row(bc_ref, bidx, ns)
            c_row = _get_row(cc_ref, bidx, ns)
            h = s_ref[0, j, hs].reshape(GROUP_W, SSM_STATE)
            h_new = d_col * h + x_col * b_row
            so_ref[0, j, hs] = h_new.reshape(SSM_HPG, SSM_HEADDIM, SSM_STATE)
            y_col = jnp.sum(h_new * c_row, axis=1, keepdims=True)
            yt_ref[rs, :] = jnp.where(m, y_col, yt_ref[rs, :])
        return carry

    lax.fori_loop(0, SSM_STEP_NB, body, 0)


def _ssm_step(dect, xdtt, bc, cc, state, out_state, l):
    nb = SSM_STEP_NB
    full = lambda shape: pl.BlockSpec(shape, lambda i: tuple(0 for _ in shape))
    sspec = pl.BlockSpec((1, nb, SSM_HEADS, SSM_HEADDIM, SSM_STATE), lambda i: (l, i, 0, 0, 0))
    in_specs = [full(dect.shape), full(xdtt.shape), full(bc.shape), full(cc.shape), sspec]
    args = [dect, xdtt, bc, cc, state]
    aliases = {}
    if out_state is not None:
        in_specs.append(pl.BlockSpec(memory_space=pl.ANY))
        args.append(out_state)
        aliases = {len(args) - 1: 1}

    def body(*refs):
        ins, rest = refs[:5], refs[len(args):]
        _ssm_step_kernel(*ins, *rest)

    return pl.pallas_call(
        body,
        grid=(DEC_BATCH // nb,),
        in_specs=in_specs,
        out_specs=[full((SSM_INNER, DEC_BATCH)), sspec],
        out_shape=[
            jax.ShapeDtypeStruct((SSM_INNER, DEC_BATCH), F32),
            jax.ShapeDtypeStruct(state.shape, F32),
        ],
        input_output_aliases=aliases,
        compiler_params=_cparams("arbitrary"),
        name="ssm_step",
    )(*args)


def _ssm_post_kernel(yt_ref, xc_ref, z_ref, dsk_ref, nw_ref, y_ref):
    for g in range(SSM_GROUPS):
        gs = slice(g * GROUP_W, (g + 1) * GROUP_W)
        y = jnp.concatenate(
            [yt_ref[g * GROUP_W + m * LANE:g * GROUP_W + (m + 1) * LANE, :].T
             for m in range(GROUP_W // LANE)], axis=1)
        y = (y + dsk_ref[:, gs] * xc_ref[:, gs]) * _silu(z_ref[:, gs])
        y_ref[:, gs] = _rms(y, nw_ref[:, gs])


def _ssm_post(yt, xc, proj, p, l):
    n = DEC_BATCH
    full = lambda shape: pl.BlockSpec(shape, lambda i: tuple(0 for _ in shape))
    lspec = pl.BlockSpec((1, 1, SSM_INNER), lambda i: (l, 0, 0))

    def body(yt_r, xc_r, z_r, dsk, nw, y_r):
        _ssm_post_kernel(yt_r, xc_r, z_r, dsk.at[0], nw.at[0], y_r)

    return pl.pallas_call(
        body,
        grid=(1,),
        in_specs=[full((SSM_INNER, n)), full((n, SSM_INNER)),
                  pl.BlockSpec((n, SSM_INNER), lambda i: (0, COL_Z // SSM_INNER)),
                  lspec, lspec],
        out_specs=full((n, SSM_INNER)),
        out_shape=jax.ShapeDtypeStruct((n, SSM_INNER), F32),
        compiler_params=_cparams("arbitrary"),
        name="ssm_post",
    )(yt, xc, proj, p['d_skip_x'], p['ssm_norm'])


def _merge_kernel(x_ref, oh_ref, y_ref, ga_ref, gb_ref, g1_ref, bma_ref, bmb_ref,
                  wa_ref, wb_ref, wo_ref, o_ref):
    a = _dot(oh_ref[...].astype(BF16), wa_ref[0])
    b = _dot(y_ref[...].astype(BF16), wb_ref[0])
    u = _sigmoid(ga_ref[...] + bma_ref[0]) * a + _sigmoid(gb_ref[...] + bmb_ref[0]) * b
    o_ref[...] = x_ref[...] + g1_ref[0] * _dot(u.astype(BF16), wo_ref[0])


def _merge(x, o_hg, y, proj, mod, p, l, tm, rows_per_group):
    rows = x.shape[0]
    wspec = lambda k: pl.BlockSpec((1, k, D_MODEL), lambda i: (l, 0, 0))
    return pl.pallas_call(
        _merge_kernel,
        grid=(rows // tm,),
        in_specs=[
            pl.BlockSpec((tm, D_MODEL), lambda i: (i, 0)),
            pl.BlockSpec((tm, D_MODEL), lambda i: (i, 0)),
            pl.BlockSpec((tm, SSM_INNER), lambda i: (i, 0)),
            pl.BlockSpec((tm, D_MODEL), lambda i: (i, COL_GA // D_MODEL)),
            pl.BlockSpec((tm, D_MODEL), lambda i: (i, COL_GB // D_MODEL)),
            _mod_spec(mod, tm, rows_per_group, ADA_G1),
            pl.BlockSpec((1, 1, D_MODEL), lambda i: (l, 0, 0)),
            pl.BlockSpec((1, 1, D_MODEL), lambda i: (l, 0, 1)),
            wspec(D_MODEL), wspec(SSM_INNER), wspec(D_MODEL),
        ],
        out_specs=pl.BlockSpec((tm, D_MODEL), lambda i: (i, 0)),
        out_shape=jax.ShapeDtypeStruct((rows, D_MODEL), F32),
        compiler_params=_cparams("arbitrary"),
        name="merge",
    )(x, o_hg, y, proj, proj, mod, p['b_merge'], p['b_merge'], p['w_br_a'], p['w_br_b'],
      p['w_out'])


FF_CHUNK = 1024


def _mlp_kernel(x_ref, sc_ref, sh_ref, g_ref, nw_ref, wu_ref, wd_ref, fw_ref, o_ref, *, final):
    x = x_ref[...]
    h = (_rms(x, nw_ref[0]) * (1.0 + sc_ref[0]) + sh_ref[0]).astype(BF16)
    acc = jnp.zeros(x.shape, F32)
    for c in range(D_FF // FF_CHUNK):
        cs = slice(c * FF_CHUNK, (c + 1) * FF_CHUNK)
        a = jnp.maximum(_dot(h, wu_ref[0, :, cs]), 0.0)
        acc = acc + _dot((a * a).astype(BF16), wd_ref[0, cs, :])
    out = x + g_ref[0] * acc
    if final:
        out = _rms(out, fw_ref[...])
    o_ref[...] = out


def _mlp(x, mod, p, norm_final, l, tm, rows_per_group):
    rows = x.shape[0]
    return pl.pallas_call(
        functools.partial(_mlp_kernel, final=(l == DEPTH - 1)),
        grid=(rows // tm,),
        in_specs=[
            pl.BlockSpec((tm, D_MODEL), lambda i: (i, 0)),
            _mod_spec(mod, tm, rows_per_group, ADA_SC2),
            _mod_spec(mod, tm, rows_per_group, ADA_SH2),
            _mod_spec(mod, tm, rows_per_group, ADA_G2),
            pl.BlockSpec((1, 1, D_MODEL), lambda i: (l, 0, 0)),
            pl.BlockSpec((1, D_MODEL, D_FF), lambda i: (l, 0, 0)),
            pl.BlockSpec((1, D_FF, D_MODEL), lambda i: (l, 0, 0)),
            pl.BlockSpec((1, D_MODEL), lambda i: (0, 0)),
        ],
        out_specs=pl.BlockSpec((tm, D_MODEL), lambda i: (i, 0)),
        out_shape=jax.ShapeDtypeStruct((rows, D_MODEL), F32),
        compiler_params=_cparams("arbitrary"),
        name="mlp",
    )(x, mod, mod, mod, p['norm_mlp'], p['w_up'], p['w_down'], norm_final)


def _prepare_params(w_ada, b_ada, norm_mix, w_in, b_merge, hg_norm, conv_w, conv_b, dt_bias,
                    a_log, d_skip, ssm_norm, w_br_a, w_br_b, w_out, norm_mlp, w_up, w_down):
    dt0 = COL_XBC + SSM_CONV_CH
    w_main = jnp.concatenate([w_in[:, :, :dt0], w_in[:, :, dt0 + SSM_HEADS:]], axis=-1)
    w_dt = jnp.pad(w_in[:, :, dt0:dt0 + SSM_HEADS], ((0, 0), (0, 0), (0, DT_PAD - SSM_HEADS)))
    per_head = lambda a: jnp.repeat(a, SSM_HEADDIM, axis=-1).reshape(DEPTH, 1, SSM_INNER)
    head_of_col = jnp.arange(SSM_INNER) // SSM_HEADDIM
    expand_all = (jnp.arange(DT_PAD)[:, None] == head_of_col[None, :]).astype(BF16)
    expand_g = expand_all.reshape(DT_PAD, SSM_GROUPS, GROUP_W).transpose(1, 0, 2)
    return {
        'norm_mix': norm_mix.reshape(DEPTH, 1, D_MODEL),
        'w_main': w_main.astype(BF16), 'w_dt': w_dt.astype(BF16),
        'b_merge': b_merge.reshape(DEPTH, 1, 2 * D_MODEL),
        'hg_norm': hg_norm.reshape(DEPTH, 1, HG_DV),
        'conv_w': conv_w, 'conv_b': conv_b.reshape(DEPTH, 1, SSM_CONV_CH),
        'dt_bias_x': per_head(dt_bias), 'a_log_x': per_head(a_log), 'd_skip_x': per_head(d_skip),
        'ssm_norm': ssm_norm.reshape(DEPTH, 1, SSM_INNER),
        'expand_all': expand_all, 'expand_g': expand_g,
        'w_br_a': w_br_a.astype(BF16), 'w_br_b': w_br_b.astype(BF16),
        'w_out': w_out.astype(BF16),
        'norm_mlp': norm_mlp.reshape(DEPTH, 1, D_MODEL),
        'w_up': w_up.astype(BF16), 'w_down': w_down.astype(BF16),
    }


def kernel(x_prompt, x_sample, state_hgrn, state_ssm, state_conv, c_prompt, c_sample, w_ada, b_ada, norm_mix, w_in, b_merge, lower_bounds, hg_norm, conv_w, conv_b, dt_bias, a_log, d_skip, ssm_norm, w_br_a, w_br_b, w_out, norm_mlp, w_up, w_down, norm_final):
    p = _prepare_params(w_ada, b_ada, norm_mix, w_in, b_merge, hg_norm, conv_w, conv_b, dt_bias,
                        a_log, d_skip, ssm_norm, w_br_a, w_br_b, w_out, norm_mlp, w_up, w_down)
    lbs = _lower_bounds(lower_bounds).reshape(DEPTH, 1, HG_HEADS * HG_DK)
    mod = _modulation(jnp.concatenate([c_prompt, c_sample], axis=0), w_ada, b_ada)
    mod_p = mod[:, :BATCH].reshape(DEPTH, BATCH, 1, N_ADA * D_MODEL)
    mod_s = mod[:, BATCH:].reshape(DEPTH, 1, DEC_BATCH, N_ADA * D_MODEL)
    norm_final2 = norm_final.reshape(1, D_MODEL)
    conv_state = state_conv.reshape(DEPTH, DEC_BATCH, (SSM_CONV - 1) * SSM_CONV_CH)

    xp = x_prompt.reshape(BATCH * SEQ, D_MODEL)
    xs = x_sample.reshape(DEC_BATCH, D_MODEL)
    hg_p, ssm_p, cv_p, cv_s = [], [], [], []
    hg_s = ssm_s = None
    for l in range(DEPTH):
        proj, dt = _inproj(xp, mod_p[l], p['norm_mix'], p['w_main'], p['w_dt'], l, 1024, SEQ)
        o_hg, s_new = _hgrn_prompt(proj, lbs, p['hg_norm'], l)
        y, h_new = _ssd_prompt(proj, dt, p, l)
        hg_p.append(s_new)
        ssm_p.append(h_new)
        cv_p.append(proj.reshape(BATCH, SEQ, PROJ_W)[:, SEQ - (SSM_CONV - 1):,
                                                       COL_XBC:COL_XBC + SSM_CONV_CH])
        xp = _merge(xp, o_hg, y, proj, mod_p[l], p, l, 512, SEQ)
        xp = _mlp(xp, mod_p[l], p, norm_final2, l, 512, SEQ)
        proj, dt = _inproj(xs, mod_s[l], p['norm_mix'], p['w_main'], p['w_dt'], l,
                           DEC_BATCH, DEC_BATCH)
        o_hg, hg_s = _hgrn_step(proj, state_hgrn, lbs, p['hg_norm'], hg_s, l)
        xc, bc, cc, dect, xdtt, cnew = _ssm_prep(proj, dt, conv_state, p, l)
        yt, ssm_s = _ssm_step(dect, xdtt, bc, cc, state_ssm, ssm_s, l)
        y = _ssm_post(yt, xc, proj, p, l)
        cv_s.append(cnew.reshape(DEC_BATCH, SSM_CONV - 1, SSM_CONV_CH))
        xs = _merge(xs, o_hg, y, proj, mod_s[l], p, l, DEC_BATCH, DEC_BATCH)
        xs = _mlp(xs, mod_s[l], p, norm_final2, l, DEC_BATCH, DEC_BATCH)
    return (xp.reshape(BATCH, SEQ, D_MODEL), xs.reshape(DEC_BATCH, 1, D_MODEL),
            jnp.stack(hg_p), jnp.stack(ssm_p), jnp.stack(cv_p),
            hg_s, ssm_s, jnp.stack(cv_s))
```

```python
import functools

import jax
import jax.numpy as jnp
import numpy as np
from jax import lax
from jax.experimental import pallas as pl
from jax.experimental.pallas import tpu as pltpu

F32 = jnp.float32
BF16 = jnp.bfloat16

D_MODEL = 1024
BATCH = 8
SEQ = 2048
DEPTH = 4
DEC_BATCH = 128
HG_HEADS = 8
HG_DK = 128
HG_DV = 128
SSM_INNER = 2048
SSM_HEADDIM = 64
SSM_HEADS = 32
SSM_GROUPS = 8
SSM_HPG = 4
SSM_STATE = 128
SSM_CONV = 4
SSM_CONV_CH = 4096
GROUP_W = SSM_HPG * SSM_HEADDIM
D_FF = 4096
N_ADA = 6
EPS = 1e-6
DT_PAD = 128
COL_Q, COL_F, COL_I, COL_OG = 0, 1024, 2048, 3072
COL_Z, COL_XBC, COL_GA, COL_GB = 4096, 6144, 10240, 11264
PROJ_W = 12288
ADA_SH1, ADA_SC1, ADA_G1, ADA_SH2, ADA_SC2, ADA_G2 = range(6)

LANE = 128
TILE = 128
VMEM_LIMIT = 56 * 1024 * 1024


def _cparams(*sem):
    return pltpu.CompilerParams(dimension_semantics=sem, vmem_limit_bytes=VMEM_LIMIT)


def _sigmoid(x):
    return 0.5 * jnp.tanh(0.5 * x) + 0.5


def _silu(x):
    return x * _sigmoid(x)


def _softplus(x):
    return jnp.maximum(x, 0.0) + jnp.log1p(jnp.exp(-jnp.abs(x)))


def _dot(a, b):
    return jnp.dot(a, b, preferred_element_type=F32)


def _dot_nt(a, b):
    return lax.dot_general(a, b, (((1,), (1,)), ((), ())), preferred_element_type=F32)


def _split(a, terms):
    parts = []
    for _ in range(terms - 1):
        hi = a.astype(BF16)
        parts.append(hi)
        a = a - hi.astype(F32)
    parts.append(a.astype(BF16))
    return parts


def _sel_dot(sel, a, terms=3):
    return functools.reduce(jnp.add, [_dot(sel, part) for part in _split(a, terms)])


def _dot_sel(a, sel, terms=3):
    return functools.reduce(jnp.add, [_dot(part, sel) for part in _split(a, terms)])


def _rms(x, w):
    ms = jnp.mean(x * x, axis=-1, keepdims=True)
    return x * lax.rsqrt(ms + EPS) * w


def _hgrn_gates(f, lb, first):
    if first:
        g = jnp.minimum(f, 0.0) - jnp.log1p(jnp.exp(-jnp.abs(f)))
        k = _sigmoid(-f)
    else:
        fg = lb + (1.0 - lb) * _sigmoid(f)
        g = jnp.log(fg)
        k = 1.0 - fg
    return g, k


def _lbs_kernel(lb_ref, o_ref):
    x = lb_ref[...]
    m = jnp.max(x, axis=0, keepdims=True)
    e = jnp.exp(x - m)
    p = e / jnp.sum(e, axis=0, keepdims=True)
    acc = jnp.zeros_like(p[0:1])
    rows = [acc]
    for l in range(1, DEPTH):
        acc = acc + p[l:l + 1]
        rows.append(acc)
    o_ref[...] = jnp.concatenate(rows, axis=0)


def _lower_bounds(lower_bounds):
    return pl.pallas_call(
        _lbs_kernel, out_shape=jax.ShapeDtypeStruct(lower_bounds.shape, F32), name="lbs",
    )(lower_bounds)


def _mod_kernel(c_ref, w_ref, b_ref, o_ref):
    a = _silu(c_ref[...]).astype(BF16)
    o_ref[0] = _dot(a, w_ref[0].astype(BF16)) + b_ref[0]


def _modulation(c_all, w_ada, b_ada):
    n = c_all.shape[0]
    tn = 1024
    return pl.pallas_call(
        _mod_kernel,
        grid=(DEPTH, N_ADA * D_MODEL // tn),
        in_specs=[
            pl.BlockSpec((n, D_MODEL), lambda l, j: (0, 0)),
            pl.BlockSpec((1, D_MODEL, tn), lambda l, j: (l, 0, j)),
            pl.BlockSpec((1, 1, tn), lambda l, j: (l, 0, j)),
        ],
        out_specs=pl.BlockSpec((1, n, tn), lambda l, j: (l, 0, j)),
        out_shape=jax.ShapeDtypeStruct((DEPTH, n, N_ADA * D_MODEL), F32),
        compiler_params=_cparams("arbitrary", "arbitrary"),
        name="modulation",
    )(c_all, w_ada, b_ada.reshape(DEPTH, 1, N_ADA * D_MODEL))


def _mod_spec(mod, tm, rows_per_group, piece):
    if mod.shape[1] == 1:
        tiles = rows_per_group // tm
        return pl.BlockSpec((1, 1, D_MODEL), lambda i, *_: (i // tiles, 0, piece))
    return pl.BlockSpec((1, tm, D_MODEL), lambda i, *_: (0, i, piece))


def _inproj_kernel(x_ref, sc_ref, sh_ref, nw_ref, w_ref, wdt_ref, dtb_ref, alog_ref,
                   o_ref, odt_ref, ocum_ref, h_scr):
    @pl.when(pl.program_id(1) == 0)
    def _():
        h = _rms(x_ref[...], nw_ref[0]) * (1.0 + sc_ref[0]) + sh_ref[0]
        hb = h.astype(BF16)
        h_scr[...] = hb
        dtv = _softplus(_dot(hb, wdt_ref[0]) + dtb_ref[0])
        odt_ref[...] = dtv
        la = dtv * -jnp.exp(alog_ref[0])
        row = lax.broadcasted_iota(jnp.int32, (TILE, TILE), 0)
        col = lax.broadcasted_iota(jnp.int32, (TILE, TILE), 1)
        tri = jnp.where(row >= col, 1.0, 0.0).astype(BF16)
        for i in range(la.shape[0] // TILE):
            rs = slice(i * TILE, (i + 1) * TILE)
            ocum_ref[rs, :] = _sel_dot(tri, la[rs])

    o_ref[...] = _dot(h_scr[...], w_ref[0])


def _inproj(x, mod, p, l, tm, rows_per_group):
    rows = x.shape[0]
    tn = 1024
    narrow = pl.BlockSpec((1, 1, DT_PAD), lambda i, j: (l, 0, 0))
    return pl.pallas_call(
        _inproj_kernel,
        grid=(rows // tm, PROJ_W // tn),
        in_specs=[
            pl.BlockSpec((tm, D_MODEL), lambda i, j: (i, 0)),
            _mod_spec(mod, tm, rows_per_group, ADA_SC1),
            _mod_spec(mod, tm, rows_per_group, ADA_SH1),
            pl.BlockSpec((1, 1, D_MODEL), lambda i, j: (l, 0, 0)),
            pl.BlockSpec((1, D_MODEL, tn), lambda i, j: (l, 0, j)),
            pl.BlockSpec((1, D_MODEL, DT_PAD), lambda i, j: (l, 0, 0)),
            narrow, narrow,
        ],
        out_specs=[
            pl.BlockSpec((tm, tn), lambda i, j: (i, j)),
            pl.BlockSpec((tm, DT_PAD), lambda i, j: (i, 0)),
            pl.BlockSpec((tm, DT_PAD), lambda i, j: (i, 0)),
        ],
        out_shape=[
            jax.ShapeDtypeStruct((rows, PROJ_W), F32),
            jax.ShapeDtypeStruct((rows, DT_PAD), F32),
            jax.ShapeDtypeStruct((rows, DT_PAD), F32),
        ],
        scratch_shapes=[pltpu.VMEM((tm, D_MODEL), BF16)],
        compiler_params=_cparams("arbitrary", "arbitrary"),
        name="inproj",
    )(x, mod, mod, p['norm_mix'], p['w_main'], p['w_dt'], p['dt_bias_n'], p['a_log_n'])


HG_LEVELS = 7
HG_HB = 4
HG_SPLIT = 2
LOG2E = 1.4426950408889634


def _hgrn_sum_ranges():
    r = np.arange(TILE)[:, None]
    c = np.arange(TILE)[None, :]
    mats = [c <= r, c > r]
    for s in range(HG_LEVELS):
        h = 1 << s
        right = ((r >> s) & 1) == 1
        in_right = (c >= (r & ~(h - 1))) & (c <= r)
        in_left = (c > r) & (c <= (r | (h - 1)))
        mats.append(np.where(right, in_right, in_left))
    m = np.concatenate(mats, axis=0).astype(np.float32)
    return np.tile(m, (1, HG_SPLIT))


def _hgrn_level_masks():
    r = np.arange(TILE)[:, None]
    c = np.arange(TILE)[None, :]
    x = r ^ c
    top = np.where(x > 0, np.floor(np.log2(np.maximum(x, 1))).astype(np.int64), -1)
    masks = [r == c] + [(r > c) & (top == s) for s in range(HG_LEVELS)]
    return np.stack(masks).astype(np.float32)


def _hgrn_prompt_kernel(q_ref, f_ref, v_ref, og_ref, lb_ref, nw_ref, sums_ref, mask_ref,
                        o_ref, s_ref, st_scr, *, first, seq):
    st_scr[...] = jnp.zeros_like(st_scr)
    lb = lb_ref[0]
    nw = nw_ref[0]

    def tile_body(t, carry):
        r0 = pl.multiple_of(t * TILE, TILE)
        rows = pl.ds(r0, TILE)
        g, k = _hgrn_gates(f_ref[rows, :], lb, first)
        g2 = jnp.concatenate(_split(g * LOG2E, HG_SPLIT), axis=0)
        decay = jnp.exp2(_dot(sums_ref[...], g2))
        q = _silu(q_ref[rows, :]) * (HG_DK ** -0.5)
        v = v_ref[rows, :]
        og = _silu(og_ref[rows, :])
        q_in = (q * decay[0:TILE]).astype(BF16)
        k_out = (k * decay[TILE:2 * TILE]).astype(BF16)
        end = decay[TILE - 1:TILE]
        qb = q.astype(BF16)
        kb = k.astype(BF16)
        vb = v.astype(BF16)
        outs = []
        for hh in range(HG_HB):
            cs = slice(hh * HG_DK, (hh + 1) * HG_DK)
            st = st_scr[hh]
            p = mask_ref[0] * _dot_nt(qb[:, cs], kb[:, cs])
            for s in range(HG_LEVELS):
                a = decay[(2 + s) * TILE:(3 + s) * TILE, cs].astype(BF16)
                p = p + mask_ref[1 + s] * _dot_nt(qb[:, cs] * a, kb[:, cs] * a)
            o = _dot_nt(q_in[:, cs], st.astype(BF16)) + _dot(p.astype(BF16), vb[:, cs])
            st_scr[hh] = st * end[:, cs] + _dot(v[:, cs].T.astype(BF16), k_out[:, cs])
            outs.append(_rms(o, nw) * og[:, cs])
        o_ref[rows, :] = jnp.concatenate(outs, axis=1).astype(o_ref.dtype)
        return carry

    lax.fori_loop(0, seq // TILE, tile_body, 0, unroll=2)
    for hh in range(HG_HB):
        s_ref[0, hh] = st_scr[hh].T


def _hgrn_prompt(proj, lbs, hg_norm, l):
    seq = SEQ
    width = HG_HB * HG_DK
    blk = lambda c0: pl.BlockSpec((seq, width), lambda b, h: (b, c0 // width + h))
    sums = jnp.asarray(_hgrn_sum_ranges(), BF16)
    masks = jnp.asarray(_hgrn_level_masks(), F32)
    return pl.pallas_call(
        functools.partial(_hgrn_prompt_kernel, first=(l == 0), seq=seq),
        grid=(BATCH, HG_HEADS // HG_HB),
        in_specs=[
            blk(COL_Q), blk(COL_F), blk(COL_I), blk(COL_OG),
            pl.BlockSpec((1, 1, width), lambda b, h: (l, 0, h)),
            pl.BlockSpec((1, 1, HG_DV), lambda b, h: (l, 0, 0)),
            pl.BlockSpec(sums.shape, lambda b, h: (0, 0)),
            pl.BlockSpec(masks.shape, lambda b, h: (0, 0, 0)),
        ],
        out_specs=[
            pl.BlockSpec((seq, width), lambda b, h: (b, h)),
            pl.BlockSpec((1, HG_HB, HG_DK, HG_DV), lambda b, h: (b, h, 0, 0)),
        ],
        out_shape=[
            jax.ShapeDtypeStruct((BATCH * seq, HG_HEADS * HG_DV), BF16),
            jax.ShapeDtypeStruct((BATCH, HG_HEADS, HG_DK, HG_DV), F32),
        ],
        scratch_shapes=[pltpu.VMEM((HG_HB, HG_DV, HG_DK), F32)],
        compiler_params=_cparams("arbitrary", "arbitrary"),
        name="hgrn_prompt",
    )(proj, proj, proj, proj, lbs, hg_norm, sums, masks)


def _conv_silu_tile(ref, w_ref, b_ref, r0, t):
    cur = ref[pl.ds(r0, TILE), :]
    prev = ref[pl.ds(pl.multiple_of(jnp.maximum(r0 - 8, 0), 8), 8), :]
    prev = jnp.where(t > 0, prev, 0.0)
    ext = jnp.concatenate([prev, cur], axis=0)
    acc = b_ref[...] + w_ref[SSM_CONV - 1:SSM_CONV, :] * cur
    for j in range(SSM_CONV - 1):
        acc = acc + w_ref[j:j + 1, :] * ext[5 + j:5 + j + TILE, :]
    return _silu(acc)


def _ssd_prompt_kernel(x_ref, bm_ref, cm_ref, z_ref, dt_ref, cum_ref,
                       wx_ref, wb_ref, wc_ref, cbx_ref, cbb_ref, cbc_ref,
                       dsk_ref, nw_ref, exp_ref,
                       y_ref, h_ref, ht_scr, *, seq):
    ht_scr[...] = jnp.zeros_like(ht_scr)
    row = lax.broadcasted_iota(jnp.int32, (TILE, TILE), 0)
    col = lax.broadcasted_iota(jnp.int32, (TILE, TILE), 1)
    causal = row >= col
    head_of_lane = lax.broadcasted_iota(jnp.int32, (TILE, GROUP_W), 1) // SSM_HEADDIM
    expand = exp_ref[0]

    def tile_body(t, carry):
        r0 = pl.multiple_of(t * TILE, TILE)
        rows = pl.ds(r0, TILE)
        xc = _conv_silu_tile(x_ref, wx_ref, cbx_ref, r0, t)
        bc = _conv_silu_tile(bm_ref, wb_ref, cbb_ref, r0, t)
        cc = _conv_silu_tile(cm_ref, wc_ref, cbc_ref, r0, t)
        dtv = _dot_sel(dt_ref[rows, :], expand, terms=2)
        cum = _dot_sel(cum_ref[rows, :], expand)
        cum_end = cum[TILE - 1:TILE, :]
        cum_t = cum.T
        xdt = xc * dtv
        xdtb = xdt.astype(BF16)
        ccb = cc.astype(BF16)
        cb = jnp.where(causal, _dot_nt(ccb, bc.astype(BF16)), 0.0)
        y = _dot(ccb, ht_scr[...].astype(BF16)) * jnp.exp(cum)
        for k in range(SSM_HPG):
            c_col = cum[:, k * SSM_HEADDIM:k * SSM_HEADDIM + 1]
            c_row = cum_t[k * SSM_HEADDIM:k * SSM_HEADDIM + 1, :]
            decay = jnp.exp(jnp.minimum(c_col - c_row, 0.0))
            m_k = (cb * decay).astype(BF16)
            x_k = jnp.where(head_of_lane == k, xdtb, jnp.zeros_like(xdtb))
            y = y + _dot(m_k, x_k)
        xs = (xdt * jnp.exp(cum_end - cum)).astype(BF16)
        ht_scr[...] = ht_scr[...] * jnp.exp(cum_end) + _dot(bc.T.astype(BF16), xs)
        y = (y + dsk_ref[...] * xc) * _silu(z_ref[rows, :])
        y_ref[rows, :] = _rms(y, nw_ref[...]).astype(y_ref.dtype)
        return carry

    lax.fori_loop(0, seq // TILE, tile_body, 0, unroll=2)
    h_ref[0] = ht_scr[...].T.reshape(SSM_HPG, SSM_HEADDIM, SSM_STATE)


def _ssd_prompt(proj, dt, cum, p, l):
    seq = SEQ
    xbc = COL_XBC
    gspec = lambda width, c0: pl.BlockSpec((seq, width), lambda b, g: (b, c0 // width + g))
    wspec = lambda rows, width, c0: pl.BlockSpec(
        (1, rows, width), lambda b, g: (l, 0, c0 // width + g))
    pspec = pl.BlockSpec((1, 1, GROUP_W), lambda b, g: (l, 0, g))
    args = (proj, proj, proj, proj, dt, cum,
            p['conv_w'], p['conv_w'], p['conv_w'], p['conv_b'], p['conv_b'], p['conv_b'],
            p['d_skip_x'], p['ssm_norm'], p['expand_g'])
    in_specs = [
        gspec(GROUP_W, xbc), gspec(SSM_STATE, xbc + SSM_INNER),
        gspec(SSM_STATE, xbc + SSM_INNER + SSM_GROUPS * SSM_STATE), gspec(GROUP_W, COL_Z),
        pl.BlockSpec((seq, DT_PAD), lambda b, g: (b, 0)),
        pl.BlockSpec((seq, DT_PAD), lambda b, g: (b, 0)),
        wspec(SSM_CONV, GROUP_W, 0), wspec(SSM_CONV, SSM_STATE, SSM_INNER),
        wspec(SSM_CONV, SSM_STATE, SSM_INNER + SSM_GROUPS * SSM_STATE),
        wspec(1, GROUP_W, 0), wspec(1, SSM_STATE, SSM_INNER),
        wspec(1, SSM_STATE, SSM_INNER + SSM_GROUPS * SSM_STATE),
        pspec, pspec,
        pl.BlockSpec((1, DT_PAD, GROUP_W), lambda b, g: (g, 0, 0)),
    ]

    def body(*refs):
        ins = list(refs[:15])
        for i in range(6, 14):
            ins[i] = ins[i].at[0]
        _ssd_prompt_kernel(*ins, *refs[15:], seq=seq)

    return pl.pallas_call(
        body,
        grid=(BATCH, SSM_GROUPS),
        in_specs=in_specs,
        out_specs=[
            pl.BlockSpec((seq, GROUP_W), lambda b, g: (b, g)),
            pl.BlockSpec((1, SSM_HPG, SSM_HEADDIM, SSM_STATE), lambda b, g: (b, g, 0, 0)),
        ],
        out_shape=[
            jax.ShapeDtypeStruct((BATCH * seq, SSM_INNER), BF16),
            jax.ShapeDtypeStruct((BATCH, SSM_HEADS, SSM_HEADDIM, SSM_STATE), F32),
        ],
        scratch_shapes=[pltpu.VMEM((SSM_STATE, GROUP_W), F32)],
        compiler_params=_cparams("arbitrary", "arbitrary"),
        name="ssd_prompt",
    )(*args)


HG_STEP_NB = 8


SUBLANES = 8


def _column(mat_t, lane_mask):
    return jnp.sum(jnp.where(lane_mask, mat_t, 0.0), axis=1, keepdims=True)


def _row_group(ref, bidx, cols):
    start = pl.multiple_of((bidx // SUBLANES) * SUBLANES, SUBLANES)
    idx = (pl.ds(start, SUBLANES), cols)
    blk = ref[idx]
    is_row = lax.broadcasted_iota(jnp.int32, blk.shape, 0) == bidx % SUBLANES
    return idx, blk, is_row


def _get_row(ref, bidx, cols):
    _, blk, is_row = _row_group(ref, bidx, cols)
    return jnp.sum(jnp.where(is_row, blk, 0.0), axis=0, keepdims=True)


def _set_row(ref, bidx, cols, row):
    idx, blk, is_row = _row_group(ref, bidx, cols)
    ref[idx] = jnp.where(is_row, row, blk)


def _hgrn_step_kernel(q_ref, f_ref, v_ref, og_ref, lb_ref, nw_ref, s_ref, o_ref, so_ref,
                      dec_t, k_t, q_t, *, first):
    i = pl.program_id(0)

    @pl.when(i == 0)
    def _():
        g, k = _hgrn_gates(f_ref[...], lb_ref[0], first)
        dec = jnp.exp(g)
        q = _silu(q_ref[...]) * (HG_DK ** -0.5)
        o_ref[...] = jnp.zeros_like(o_ref)
        for h in range(HG_HEADS):
            hs = slice(h * HG_DK, (h + 1) * HG_DK)
            dec_t[hs, :] = dec[:, hs].T
            k_t[hs, :] = k[:, hs].T
            q_t[hs, :] = q[:, hs].T

    lane = lax.broadcasted_iota(jnp.int32, (HG_DK, DEC_BATCH), 1)

    def body(j, carry):
        bidx = i * HG_STEP_NB + j
        m = lane == bidx
        for h in range(HG_HEADS):
            hs = slice(h * HG_DK, (h + 1) * HG_DK)
            d_col = _column(dec_t[hs, :], m)
            k_col = _column(k_t[hs, :], m)
            q_col = _column(q_t[hs, :], m)
            v_row = _get_row(v_ref, bidx, hs)
            s_new = d_col * s_ref[0, j, h] + k_col * v_row
            so_ref[0, j, h] = s_new
            o_row = jnp.sum(q_col * s_new, axis=0, keepdims=True)
            _set_row(o_ref, bidx, hs, _rms(o_row, nw_ref[0]) * _silu(_get_row(og_ref, bidx, hs)))
        return carry

    lax.fori_loop(0, HG_STEP_NB, body, 0)


def _hgrn_step(proj, state, lbs, hg_norm, out_state, l):
    nb = HG_STEP_NB
    blk = lambda c0: pl.BlockSpec((DEC_BATCH, D_MODEL), lambda i: (0, c0 // D_MODEL))
    sspec = pl.BlockSpec((1, nb, HG_HEADS, HG_DK, HG_DV), lambda i: (l, i, 0, 0, 0))
    in_specs = [
        blk(COL_Q), blk(COL_F), blk(COL_I), blk(COL_OG),
        pl.BlockSpec((1, 1, D_MODEL), lambda i: (l, 0, 0)),
        pl.BlockSpec((1, 1, HG_DV), lambda i: (l, 0, 0)),
        sspec,
    ]
    args = [proj, proj, proj, proj, lbs, hg_norm, state]
    aliases = {}
    if out_state is not None:
        in_specs.append(pl.BlockSpec(memory_space=pl.ANY))
        args.append(out_state)
        aliases = {len(args) - 1: 1}

    def body(*refs):
        ins, rest = refs[:7], refs[len(args):]
        _hgrn_step_kernel(*ins, *rest, first=(l == 0))

    return pl.pallas_call(
        body,
        grid=(DEC_BATCH // nb,),
        in_specs=in_specs,
        out_specs=[pl.BlockSpec((DEC_BATCH, D_MODEL), lambda i: (0, 0)), sspec],
        out_shape=[
            jax.ShapeDtypeStruct((DEC_BATCH, D_MODEL), F32),
            jax.ShapeDtypeStruct(state.shape, F32),
        ],
        scratch_shapes=[pltpu.VMEM((D_MODEL, DEC_BATCH), F32)] * 3,
        input_output_aliases=aliases,
        compiler_params=_cparams("arbitrary"),
        name="hgrn_step",
    )(*args)


def _ssm_prep_kernel(xs_ref, bm_ref, cm_ref, dt_ref, cst_ref, w_ref, cb_ref,
                     alog_ref, exp_ref,
                     xc_ref, bc_ref, cc_ref, dect_ref, xdtt_ref, cnew_ref):
    ch = SSM_CONV_CH
    parts = ((xs_ref, 0, SSM_INNER, xc_ref),
             (bm_ref, SSM_INNER, SSM_INNER + 1024, bc_ref),
             (cm_ref, SSM_INNER + 1024, ch, cc_ref))
    for u_ref, c0, c1, out_ref in parts:
        u = u_ref[...]
        acc = cb_ref[:, c0:c1] + w_ref[SSM_CONV - 1:SSM_CONV, c0:c1] * u
        for j in range(SSM_CONV - 1):
            acc = acc + w_ref[j:j + 1, c0:c1] * cst_ref[:, j * ch + c0:j * ch + c1]
        out_ref[...] = _silu(acc)
        cnew_ref[:, (SSM_CONV - 2) * ch + c0:(SSM_CONV - 2) * ch + c1] = u
    cnew_ref[:, 0:(SSM_CONV - 2) * ch] = cst_ref[:, ch:(SSM_CONV - 1) * ch]
    dtv = _dot_sel(dt_ref[...], exp_ref[...])
    dec = jnp.exp(dtv * -jnp.exp(alog_ref[...]))
    xdt = xc_ref[...] * dtv
    for m in range(SSM_INNER // LANE):
        ms = slice(m * LANE, (m + 1) * LANE)
        dect_ref[ms, :] = dec[:, ms].T
        xdtt_ref[ms, :] = xdt[:, ms].T


def _ssm_prep(proj, dt, conv_state, p, l):
    n = DEC_BATCH
    full = lambda shape: pl.BlockSpec(shape, lambda i: tuple(0 for _ in shape))
    lspec = lambda rows, width: pl.BlockSpec((1, rows, width), lambda i: (l, 0, 0))

    def body(xs, bm, cm, dtr, cst, w, cb, alog, ex, *outs):
        _ssm_prep_kernel(xs, bm, cm, dtr, cst.at[0], w.at[0], cb.at[0], alog.at[0], ex, *outs)

    return pl.pallas_call(
        body,
        grid=(1,),
        in_specs=[
            pl.BlockSpec((n, SSM_INNER), lambda i: (0, COL_XBC // SSM_INNER)),
            pl.BlockSpec((n, 1024), lambda i: (0, (COL_XBC + SSM_INNER) // 1024)),
            pl.BlockSpec((n, 1024), lambda i: (0, (COL_XBC + SSM_INNER) // 1024 + 1)),
            full((n, DT_PAD)),
            pl.BlockSpec((1, n, (SSM_CONV - 1) * SSM_CONV_CH), lambda i: (l, 0, 0)),
            lspec(SSM_CONV, SSM_CONV_CH), lspec(1, SSM_CONV_CH),
            lspec(1, SSM_INNER),
            full((DT_PAD, SSM_INNER)),
        ],
        out_specs=[
            full((n, SSM_INNER)), full((n, 1024)), full((n, 1024)),
            full((SSM_INNER, n)), full((SSM_INNER, n)),
            full((n, (SSM_CONV - 1) * SSM_CONV_CH)),
        ],
        out_shape=[
            jax.ShapeDtypeStruct((n, SSM_INNER), F32),
            jax.ShapeDtypeStruct((n, 1024), F32),
            jax.ShapeDtypeStruct((n, 1024), F32),
            jax.ShapeDtypeStruct((SSM_INNER, n), F32),
            jax.ShapeDtypeStruct((SSM_INNER, n), F32),
            jax.ShapeDtypeStruct((n, (SSM_CONV - 1) * SSM_CONV_CH), F32),
        ],
        compiler_params=_cparams("arbitrary"),
        name="ssm_prep",
    )(proj, proj, proj, dt, conv_state, p['conv_w'], p['conv_b'], p['a_log_x'], p['expand_all'])


SSM_STEP_NB = 4


def _ssm_step_kernel(dect_ref, xdtt_ref, bc_ref, cc_ref, s_ref, yt_ref, so_ref):
    i = pl.program_id(0)
    lane = lax.broadcasted_iota(jnp.int32, (GROUP_W, DEC_BATCH), 1)

    @pl.when(i == 0)
    def _():
        yt_ref[...] = jnp.zeros_like(yt_ref)

    def body(j, carry):
        bidx = i * SSM_STEP_NB + j
        m = lane == bidx
        for g in range(SSM_GROUPS):
            rs = slice(g * GROUP_W, (g + 1) * GROUP_W)
            hs = slice(g * SSM_HPG, (g + 1) * SSM_HPG)
            ns = slice(g * SSM_STATE, (g + 1) * SSM_STATE)
            d_col = _column(dect_ref[rs, :], m)
            x_col = _column(xdtt_ref[rs, :], m)
            b_row = _get_row(bc_ref, bidx, ns)
            c_row = _get_row(cc_ref, bidx, ns)
            h = s_ref[0, j, hs].reshape(GROUP_W, SSM_STATE)
            h_new = d_col * h + x_col * b_row
            so_ref[0, j, hs] = h_new.reshape(SSM_HPG, SSM_HEADDIM, SSM_STATE)
            y_col = jnp.sum(h_new * c_row, axis=1, keepdims=True)
            yt_ref[rs, :] = jnp.where(m, y_col, yt_ref[rs, :])
        return carry

    lax.fori_loop(0, SSM_STEP_NB, body, 0)


def _ssm_step(dect, xdtt, bc, cc, state, out_state, l):
    nb = SSM_STEP_NB
    full = lambda shape: pl.BlockSpec(shape, lambda i: tuple(0 for _ in shape))
    sspec = pl.BlockSpec((1, nb, SSM_HEADS, SSM_HEADDIM, SSM_STATE), lambda i: (l, i, 0, 0, 0))
    in_specs = [full(dect.shape), full(xdtt.shape), full(bc.shape), full(cc.shape), sspec]
    args = [dect, xdtt, bc, cc, state]
    aliases = {}
    if out_state is not None:
        in_specs.append(pl.BlockSpec(memory_space=pl.ANY))
        args.append(out_state)
        aliases = {len(args) - 1: 1}

    def body(*refs):
        ins, rest = refs[:5], refs[len(args):]
        _ssm_step_kernel(*ins, *rest)

    return pl.pallas_call(
        body,
        grid=(DEC_BATCH // nb,),
        in_specs=in_specs,
        out_specs=[full((SSM_INNER, DEC_BATCH)), sspec],
        out_shape=[
            jax.ShapeDtypeStruct((SSM_INNER, DEC_BATCH), F32),
            jax.ShapeDtypeStruct(state.shape, F32),
        ],
        input_output_aliases=aliases,
        compiler_params=_cparams("arbitrary"),
        name="ssm_step",
    )(*args)


def _ssm_post_kernel(yt_ref, xc_ref, z_ref, dsk_ref, nw_ref, y_ref):
    for g in range(SSM_GROUPS):
        gs = slice(g * GROUP_W, (g + 1) * GROUP_W)
        y = jnp.concatenate(
            [yt_ref[g * GROUP_W + m * LANE:g * GROUP_W + (m + 1) * LANE, :].T
             for m in range(GROUP_W // LANE)], axis=1)
        y = (y + dsk_ref[:, gs] * xc_ref[:, gs]) * _silu(z_ref[:, gs])
        y_ref[:, gs] = _rms(y, nw_ref[:, gs])


def _ssm_post(yt, xc, proj, p, l):
    n = DEC_BATCH
    full = lambda shape: pl.BlockSpec(shape, lambda i: tuple(0 for _ in shape))
    lspec = pl.BlockSpec((1, 1, SSM_INNER), lambda i: (l, 0, 0))

    def body(yt_r, xc_r, z_r, dsk, nw, y_r):
        _ssm_post_kernel(yt_r, xc_r, z_r, dsk.at[0], nw.at[0], y_r)

    return pl.pallas_call(
        body,
        grid=(1,),
        in_specs=[full((SSM_INNER, n)), full((n, SSM_INNER)),
                  pl.BlockSpec((n, SSM_INNER), lambda i: (0, COL_Z // SSM_INNER)),
                  lspec, lspec],
        out_specs=full((n, SSM_INNER)),
        out_shape=jax.ShapeDtypeStruct((n, SSM_INNER), F32),
        compiler_params=_cparams("arbitrary"),
        name="ssm_post",
    )(yt, xc, proj, p['d_skip_x'], p['ssm_norm'])


def _merge_kernel(x_ref, oh_ref, y_ref, ga_ref, gb_ref, g1_ref, bma_ref, bmb_ref,
                  wa_ref, wb_ref, wo_ref, o_ref):
    a = _dot(oh_ref[...].astype(BF16), wa_ref[0])
    b = _dot(y_ref[...].astype(BF16), wb_ref[0])
    u = _sigmoid(ga_ref[...] + bma_ref[0]) * a + _sigmoid(gb_ref[...] + bmb_ref[0]) * b
    o_ref[...] = x_ref[...] + g1_ref[0] * _dot(u.astype(BF16), wo_ref[0])


def _merge(x, o_hg, y, proj, mod, p, l, tm, rows_per_group):
    rows = x.shape[0]
    wspec = lambda k: pl.BlockSpec((1, k, D_MODEL), lambda i: (l, 0, 0))
    return pl.pallas_call(
        _merge_kernel,
        grid=(rows // tm,),
        in_specs=[
            pl.BlockSpec((tm, D_MODEL), lambda i: (i, 0)),
            pl.BlockSpec((tm, D_MODEL), lambda i: (i, 0)),
            pl.BlockSpec((tm, SSM_INNER), lambda i: (i, 0)),
            pl.BlockSpec((tm, D_MODEL), lambda i: (i, COL_GA // D_MODEL)),
            pl.BlockSpec((tm, D_MODEL), lambda i: (i, COL_GB // D_MODEL)),
            _mod_spec(mod, tm, rows_per_group, ADA_G1),
            pl.BlockSpec((1, 1, D_MODEL), lambda i: (l, 0, 0)),
            pl.BlockSpec((1, 1, D_MODEL), lambda i: (l, 0, 1)),
            wspec(D_MODEL), wspec(SSM_INNER), wspec(D_MODEL),
        ],
        out_specs=pl.BlockSpec((tm, D_MODEL), lambda i: (i, 0)),
        out_shape=jax.ShapeDtypeStruct((rows, D_MODEL), F32),
        compiler_params=_cparams("arbitrary"),
        name="merge",
    )(x, o_hg, y, proj, proj, mod, p['b_merge'], p['b_merge'], p['w_br_a'], p['w_br_b'],
      p['w_out'])


FF_CHUNK = 1024


def _mlp_kernel(x_ref, sc_ref, sh_ref, g_ref, nw_ref, wu_ref, wd_ref, fw_ref, o_ref, *, final):
    x = x_ref[...]
    h = (_rms(x, nw_ref[0]) * (1.0 + sc_ref[0]) + sh_ref[0]).astype(BF16)
    acc = jnp.zeros(x.shape, F32)
    for c in range(D_FF // FF_CHUNK):
        cs = slice(c * FF_CHUNK, (c + 1) * FF_CHUNK)
        a = jnp.maximum(_dot(h, wu_ref[0, :, cs]), 0.0)
        acc = acc + _dot((a * a).astype(BF16), wd_ref[0, cs, :])
    out = x + g_ref[0] * acc
    if final:
        out = _rms(out, fw_ref[...])
    o_ref[...] = out


def _mlp(x, mod, p, norm_final, l, tm, rows_per_group):
    rows = x.shape[0]
    return pl.pallas_call(
        functools.partial(_mlp_kernel, final=(l == DEPTH - 1)),
        grid=(rows // tm,),
        in_specs=[
            pl.BlockSpec((tm, D_MODEL), lambda i: (i, 0)),
            _mod_spec(mod, tm, rows_per_group, ADA_SC2),
            _mod_spec(mod, tm, rows_per_group, ADA_SH2),
            _mod_spec(mod, tm, rows_per_group, ADA_G2),
            pl.BlockSpec((1, 1, D_MODEL), lambda i: (l, 0, 0)),
            pl.BlockSpec((1, D_MODEL, D_FF), lambda i: (l, 0, 0)),
            pl.BlockSpec((1, D_FF, D_MODEL), lambda i: (l, 0, 0)),
            pl.BlockSpec((1, D_MODEL), lambda i: (0, 0)),
        ],
        out_specs=pl.BlockSpec((tm, D_MODEL), lambda i: (i, 0)),
        out_shape=jax.ShapeDtypeStruct((rows, D_MODEL), F32),
        compiler_params=_cparams("arbitrary"),
        name="mlp",
    )(x, mod, mod, mod, p['norm_mlp'], p['w_up'], p['w_down'], norm_final)


def _prepare_params(w_ada, b_ada, norm_mix, w_in, b_merge, hg_norm, conv_w, conv_b, dt_bias,
                    a_log, d_skip, ssm_norm, w_br_a, w_br_b, w_out, norm_mlp, w_up, w_down):
    dt0 = COL_XBC + SSM_CONV_CH
    w_main = jnp.concatenate([w_in[:, :, :dt0], w_in[:, :, dt0 + SSM_HEADS:]], axis=-1)
    w_dt = jnp.pad(w_in[:, :, dt0:dt0 + SSM_HEADS], ((0, 0), (0, 0), (0, DT_PAD - SSM_HEADS)))
    per_head = lambda a: jnp.repeat(a, SSM_HEADDIM, axis=-1).reshape(DEPTH, 1, SSM_INNER)
    narrow = lambda a: jnp.pad(a, ((0, 0), (0, DT_PAD - SSM_HEADS))).reshape(DEPTH, 1, DT_PAD)
    head_of_col = jnp.arange(SSM_INNER) // SSM_HEADDIM
    expand_all = (jnp.arange(DT_PAD)[:, None] == head_of_col[None, :]).astype(BF16)
    expand_g = expand_all.reshape(DT_PAD, SSM_GROUPS, GROUP_W).transpose(1, 0, 2)
    return {
        'norm_mix': norm_mix.reshape(DEPTH, 1, D_MODEL),
        'w_main': w_main.astype(BF16), 'w_dt': w_dt.astype(BF16),
        'b_merge': b_merge.reshape(DEPTH, 1, 2 * D_MODEL),
        'hg_norm': hg_norm.reshape(DEPTH, 1, HG_DV),
        'conv_w': conv_w, 'conv_b': conv_b.reshape(DEPTH, 1, SSM_CONV_CH),
        'dt_bias_n': narrow(dt_bias), 'a_log_n': narrow(a_log),
        'a_log_x': per_head(a_log), 'd_skip_x': per_head(d_skip),
        'ssm_norm': ssm_norm.reshape(DEPTH, 1, SSM_INNER),
        'expand_all': expand_all, 'expand_g': expand_g,
        'w_br_a': w_br_a.astype(BF16), 'w_br_b': w_br_b.astype(BF16),
        'w_out': w_out.astype(BF16),
        'norm_mlp': norm_mlp.reshape(DEPTH, 1, D_MODEL),
        'w_up': w_up.astype(BF16), 'w_down': w_down.astype(BF16),
    }


def kernel(x_prompt, x_sample, state_hgrn, state_ssm, state_conv, c_prompt, c_sample, w_ada, b_ada, norm_mix, w_in, b_merge, lower_bounds, hg_norm, conv_w, conv_b, dt_bias, a_log, d_skip, ssm_norm, w_br_a, w_br_b, w_out, norm_mlp, w_up, w_down, norm_final):
    p = _prepare_params(w_ada, b_ada, norm_mix, w_in, b_merge, hg_norm, conv_w, conv_b, dt_bias,
                        a_log, d_skip, ssm_norm, w_br_a, w_br_b, w_out, norm_mlp, w_up, w_down)
    lbs = _lower_bounds(lower_bounds).reshape(DEPTH, 1, HG_HEADS * HG_DK)
    mod = _modulation(jnp.concatenate([c_prompt, c_sample], axis=0), w_ada, b_ada)
    mod_p = mod[:, :BATCH].reshape(DEPTH, BATCH, 1, N_ADA * D_MODEL)
    mod_s = mod[:, BATCH:].reshape(DEPTH, 1, DEC_BATCH, N_ADA * D_MODEL)
    norm_final2 = norm_final.reshape(1, D_MODEL)
    conv_state = state_conv.reshape(DEPTH, DEC_BATCH, (SSM_CONV - 1) * SSM_CONV_CH)

    xp = x_prompt.reshape(BATCH * SEQ, D_MODEL)
    xs = x_sample.reshape(DEC_BATCH, D_MODEL)
    hg_p, ssm_p, cv_p, cv_s = [], [], [], []
    hg_s = ssm_s = None
    for l in range(DEPTH):
        proj, dt, cum = _inproj(xp, mod_p[l], p, l, 1024, SEQ)
        o_hg, s_new = _hgrn_prompt(proj, lbs, p['hg_norm'], l)
        y, h_new = _ssd_prompt(proj, dt, cum, p, l)
        hg_p.append(s_new)
        ssm_p.append(h_new)
        cv_p.append(proj.reshape(BATCH, SEQ, PROJ_W)[:, SEQ - (SSM_CONV - 1):,
                                                       COL_XBC:COL_XBC + SSM_CONV_CH])
        xp = _merge(xp, o_hg, y, proj, mod_p[l], p, l, 512, SEQ)
        xp = _mlp(xp, mod_p[l], p, norm_final2, l, 512, SEQ)
        proj, dt, _ = _inproj(xs, mod_s[l], p, l, DEC_BATCH, DEC_BATCH)
        o_hg, hg_s = _hgrn_step(proj, state_hgrn, lbs, p['hg_norm'], hg_s, l)
        xc, bc, cc, dect, xdtt, cnew = _ssm_prep(proj, dt, conv_state, p, l)
        yt, ssm_s = _ssm_step(dect, xdtt, bc, cc, state_ssm, ssm_s, l)
        y = _ssm_post(yt, xc, proj, p, l)
        cv_s.append(cnew.reshape(DEC_BATCH, SSM_CONV - 1, SSM_CONV_CH))
        xs = _merge(xs, o_hg, y, proj, mod_s[l], p, l, DEC_BATCH, DEC_BATCH)
        xs = _mlp(xs, mod_s[l], p, norm_final2, l, DEC_BATCH, DEC_BATCH)
    return (xp.reshape(BATCH, SEQ, D_MODEL), xs.reshape(DEC_BATCH, 1, D_MODEL),
            jnp.stack(hg_p), jnp.stack(ssm_p), jnp.stack(cv_p),
            hg_s, ssm_s, jnp.stack(cv_s))
```

```python
import functools

import jax
import jax.numpy as jnp
import numpy as np
from jax import lax
from jax.experimental import pallas as pl
from jax.experimental.pallas import tpu as pltpu

F32 = jnp.float32
BF16 = jnp.bfloat16

D_MODEL = 1024
BATCH = 8
SEQ = 2048
DEPTH = 4
DEC_BATCH = 128
HG_HEADS = 8
HG_DK = 128
HG_DV = 128
SSM_INNER = 2048
SSM_HEADDIM = 64
SSM_HEADS = 32
SSM_GROUPS = 8
SSM_HPG = 4
SSM_STATE = 128
SSM_CONV = 4
SSM_CONV_CH = 4096
GROUP_W = SSM_HPG * SSM_HEADDIM
D_FF = 4096
N_ADA = 6
EPS = 1e-6
DT_PAD = 128
COL_Q, COL_F, COL_I, COL_OG = 0, 1024, 2048, 3072
COL_Z, COL_XBC, COL_GA, COL_GB = 4096, 6144, 10240, 11264
PROJ_W = 12288
ADA_SH1, ADA_SC1, ADA_G1, ADA_SH2, ADA_SC2, ADA_G2 = range(6)

LANE = 128
TILE = 128
VMEM_LIMIT = 56 * 1024 * 1024


def _cparams(*sem):
    return pltpu.CompilerParams(dimension_semantics=sem, vmem_limit_bytes=VMEM_LIMIT)


def _sigmoid(x):
    return 0.5 * jnp.tanh(0.5 * x) + 0.5


def _silu(x):
    return x * _sigmoid(x)


def _softplus(x):
    return jnp.maximum(x, 0.0) + jnp.log1p(jnp.exp(-jnp.abs(x)))


def _dot(a, b):
    return jnp.dot(a, b, preferred_element_type=F32)


def _dot_nt(a, b):
    return lax.dot_general(a, b, (((1,), (1,)), ((), ())), preferred_element_type=F32)


def _split(a, terms):
    parts = []
    for _ in range(terms - 1):
        hi = a.astype(BF16)
        parts.append(hi)
        a = a - hi.astype(F32)
    parts.append(a.astype(BF16))
    return parts


def _sel_dot(sel, a, terms=3):
    return functools.reduce(jnp.add, [_dot(sel, part) for part in _split(a, terms)])


def _dot_sel(a, sel, terms=3):
    return functools.reduce(jnp.add, [_dot(part, sel) for part in _split(a, terms)])


def _rms(x, w):
    ms = jnp.mean(x * x, axis=-1, keepdims=True)
    return x * lax.rsqrt(ms + EPS) * w


def _hgrn_gates(f, lb, first):
    if first:
        g = jnp.minimum(f, 0.0) - jnp.log1p(jnp.exp(-jnp.abs(f)))
        k = _sigmoid(-f)
    else:
        fg = lb + (1.0 - lb) * _sigmoid(f)
        g = jnp.log(fg)
        k = 1.0 - fg
    return g, k


def _lbs_kernel(lb_ref, o_ref):
    x = lb_ref[...]
    m = jnp.max(x, axis=0, keepdims=True)
    e = jnp.exp(x - m)
    p = e / jnp.sum(e, axis=0, keepdims=True)
    acc = jnp.zeros_like(p[0:1])
    rows = [acc]
    for l in range(1, DEPTH):
        acc = acc + p[l:l + 1]
        rows.append(acc)
    o_ref[...] = jnp.concatenate(rows, axis=0)


def _lower_bounds(lower_bounds):
    return pl.pallas_call(
        _lbs_kernel, out_shape=jax.ShapeDtypeStruct(lower_bounds.shape, F32), name="lbs",
    )(lower_bounds)


def _mod_kernel(c_ref, w_ref, b_ref, o_ref):
    a = _silu(c_ref[...]).astype(BF16)
    o_ref[0] = _dot(a, w_ref[0].astype(BF16)) + b_ref[0]


def _modulation(c_all, w_ada, b_ada):
    n = c_all.shape[0]
    tn = 1024
    return pl.pallas_call(
        _mod_kernel,
        grid=(DEPTH, N_ADA * D_MODEL // tn),
        in_specs=[
            pl.BlockSpec((n, D_MODEL), lambda l, j: (0, 0)),
            pl.BlockSpec((1, D_MODEL, tn), lambda l, j: (l, 0, j)),
            pl.BlockSpec((1, 1, tn), lambda l, j: (l, 0, j)),
        ],
        out_specs=pl.BlockSpec((1, n, tn), lambda l, j: (l, 0, j)),
        out_shape=jax.ShapeDtypeStruct((DEPTH, n, N_ADA * D_MODEL), F32),
        compiler_params=_cparams("arbitrary", "arbitrary"),
        name="modulation",
    )(c_all, w_ada, b_ada.reshape(DEPTH, 1, N_ADA * D_MODEL))


def _mod_spec(mod, tm, rows_per_group, piece):
    if mod.shape[1] == 1:
        tiles = rows_per_group // tm
        return pl.BlockSpec((1, 1, D_MODEL), lambda i, *_: (i // tiles, 0, piece))
    return pl.BlockSpec((1, tm, D_MODEL), lambda i, *_: (0, i, piece))


def _inproj_kernel(x_ref, sc_ref, sh_ref, nw_ref, w_ref, wdt_ref, dtb_ref, alog_ref,
                   o_ref, odt_ref, ocum_ref, h_scr):
    @pl.when(pl.program_id(1) == 0)
    def _():
        h = _rms(x_ref[...], nw_ref[0]) * (1.0 + sc_ref[0]) + sh_ref[0]
        hb = h.astype(BF16)
        h_scr[...] = hb
        dtv = _softplus(_dot(hb, wdt_ref[0]) + dtb_ref[0])
        odt_ref[...] = dtv
        la = dtv * -jnp.exp(alog_ref[0])
        row = lax.broadcasted_iota(jnp.int32, (TILE, TILE), 0)
        col = lax.broadcasted_iota(jnp.int32, (TILE, TILE), 1)
        tri = jnp.where(row >= col, 1.0, 0.0).astype(BF16)
        for i in range(la.shape[0] // TILE):
            rs = slice(i * TILE, (i + 1) * TILE)
            ocum_ref[rs, :] = _sel_dot(tri, la[rs])

    o_ref[...] = _dot(h_scr[...], w_ref[0])


def _inproj(x, mod, p, l, tm, rows_per_group):
    rows = x.shape[0]
    tn = 1024
    narrow = pl.BlockSpec((1, 1, DT_PAD), lambda i, j: (l, 0, 0))
    return pl.pallas_call(
        _inproj_kernel,
        grid=(rows // tm, PROJ_W // tn),
        in_specs=[
            pl.BlockSpec((tm, D_MODEL), lambda i, j: (i, 0)),
            _mod_spec(mod, tm, rows_per_group, ADA_SC1),
            _mod_spec(mod, tm, rows_per_group, ADA_SH1),
            pl.BlockSpec((1, 1, D_MODEL), lambda i, j: (l, 0, 0)),
            pl.BlockSpec((1, D_MODEL, tn), lambda i, j: (l, 0, j)),
            pl.BlockSpec((1, D_MODEL, DT_PAD), lambda i, j: (l, 0, 0)),
            narrow, narrow,
        ],
        out_specs=[
            pl.BlockSpec((tm, tn), lambda i, j: (i, j)),
            pl.BlockSpec((tm, DT_PAD), lambda i, j: (i, 0)),
            pl.BlockSpec((tm, DT_PAD), lambda i, j: (i, 0)),
        ],
        out_shape=[
            jax.ShapeDtypeStruct((rows, PROJ_W), F32),
            jax.ShapeDtypeStruct((rows, DT_PAD), F32),
            jax.ShapeDtypeStruct((rows, DT_PAD), F32),
        ],
        scratch_shapes=[pltpu.VMEM((tm, D_MODEL), BF16)],
        compiler_params=_cparams("arbitrary", "arbitrary"),
        name="inproj",
    )(x, mod, mod, p['norm_mix'], p['w_main'], p['w_dt'], p['dt_bias_n'], p['a_log_n'])


HG_LEVELS = 7
HG_HB = 4
HG_SPLIT = 2
LOG2E = 1.4426950408889634


def _hgrn_sum_ranges():
    r = np.arange(TILE)[:, None]
    c = np.arange(TILE)[None, :]
    mats = [c <= r, c > r]
    for s in range(HG_LEVELS):
        h = 1 << s
        right = ((r >> s) & 1) == 1
        in_right = (c >= (r & ~(h - 1))) & (c <= r)
        in_left = (c > r) & (c <= (r | (h - 1)))
        mats.append(np.where(right, in_right, in_left))
    m = np.concatenate(mats, axis=0).astype(np.float32)
    return np.tile(m, (1, HG_SPLIT))


def _hgrn_level_masks():
    r = np.arange(TILE)[:, None]
    c = np.arange(TILE)[None, :]
    x = r ^ c
    top = np.where(x > 0, np.floor(np.log2(np.maximum(x, 1))).astype(np.int64), -1)
    masks = [r == c] + [(r > c) & (top == s) for s in range(HG_LEVELS)]
    return np.stack(masks).astype(np.float32)


def _hgrn_prompt_kernel(q_ref, f_ref, v_ref, og_ref, lb_ref, nw_ref, sums_ref, mask_ref,
                        o_ref, s_ref, st_scr, *, first, seq):
    st_scr[...] = jnp.zeros_like(st_scr)
    lb = lb_ref[0]
    nw = nw_ref[0]

    def tile_body(t, carry):
        r0 = pl.multiple_of(t * TILE, TILE)
        rows = pl.ds(r0, TILE)
        g, k = _hgrn_gates(f_ref[rows, :], lb, first)
        g2 = jnp.concatenate(_split(g * LOG2E, HG_SPLIT), axis=0)
        decay = jnp.exp2(_dot(sums_ref[...], g2))
        q = _silu(q_ref[rows, :]) * (HG_DK ** -0.5)
        v = v_ref[rows, :]
        og = _silu(og_ref[rows, :])
        q_in = (q * decay[0:TILE]).astype(BF16)
        k_out = (k * decay[TILE:2 * TILE]).astype(BF16)
        end = decay[TILE - 1:TILE]
        qb = q.astype(BF16)
        kb = k.astype(BF16)
        vb = v.astype(BF16)
        outs = []
        for hh in range(HG_HB):
            cs = slice(hh * HG_DK, (hh + 1) * HG_DK)
            st = st_scr[hh]
            p = mask_ref[0] * _dot_nt(qb[:, cs], kb[:, cs])
            for s in range(HG_LEVELS):
                a = decay[(2 + s) * TILE:(3 + s) * TILE, cs].astype(BF16)
                p = p + mask_ref[1 + s] * _dot_nt(qb[:, cs] * a, kb[:, cs] * a)
            o = _dot_nt(q_in[:, cs], st.astype(BF16)) + _dot(p.astype(BF16), vb[:, cs])
            st_scr[hh] = st * end[:, cs] + _dot(v[:, cs].T.astype(BF16), k_out[:, cs])
            outs.append(_rms(o, nw) * og[:, cs])
        o_ref[rows, :] = jnp.concatenate(outs, axis=1).astype(o_ref.dtype)
        return carry

    lax.fori_loop(0, seq // TILE, tile_body, 0, unroll=2)
    for hh in range(HG_HB):
        s_ref[0, hh] = st_scr[hh].T


def _hgrn_prompt(proj, lbs, hg_norm, l):
    seq = SEQ
    width = HG_HB * HG_DK
    blk = lambda c0: pl.BlockSpec((seq, width), lambda b, h: (b, c0 // width + h))
    sums = jnp.asarray(_hgrn_sum_ranges(), BF16)
    masks = jnp.asarray(_hgrn_level_masks(), F32)
    return pl.pallas_call(
        functools.partial(_hgrn_prompt_kernel, first=(l == 0), seq=seq),
        grid=(BATCH, HG_HEADS // HG_HB),
        in_specs=[
            blk(COL_Q), blk(COL_F), blk(COL_I), blk(COL_OG),
            pl.BlockSpec((1, 1, width), lambda b, h: (l, 0, h)),
            pl.BlockSpec((1, 1, HG_DV), lambda b, h: (l, 0, 0)),
            pl.BlockSpec(sums.shape, lambda b, h: (0, 0)),
            pl.BlockSpec(masks.shape, lambda b, h: (0, 0, 0)),
        ],
        out_specs=[
            pl.BlockSpec((seq, width), lambda b, h: (b, h)),
            pl.BlockSpec((1, HG_HB, HG_DK, HG_DV), lambda b, h: (b, h, 0, 0)),
        ],
        out_shape=[
            jax.ShapeDtypeStruct((BATCH * seq, HG_HEADS * HG_DV), BF16),
            jax.ShapeDtypeStruct((BATCH, HG_HEADS, HG_DK, HG_DV), F32),
        ],
        scratch_shapes=[pltpu.VMEM((HG_HB, HG_DV, HG_DK), F32)],
        compiler_params=_cparams("arbitrary", "arbitrary"),
        name="hgrn_prompt",
    )(proj, proj, proj, proj, lbs, hg_norm, sums, masks)


def _conv_silu_tile(ref, w_ref, b_ref, r0, t):
    cur = ref[pl.ds(r0, TILE), :]
    prev = ref[pl.ds(pl.multiple_of(jnp.maximum(r0 - 8, 0), 8), 8), :]
    prev = jnp.where(t > 0, prev, 0.0)
    ext = jnp.concatenate([prev, cur], axis=0)
    acc = b_ref[...] + w_ref[SSM_CONV - 1:SSM_CONV, :] * cur
    for j in range(SSM_CONV - 1):
        acc = acc + w_ref[j:j + 1, :] * ext[5 + j:5 + j + TILE, :]
    return _silu(acc)


def _ssd_prompt_kernel(x_ref, bm_ref, cm_ref, z_ref, dt_ref, cum_ref,
                       wx_ref, wb_ref, wc_ref, cbx_ref, cbb_ref, cbc_ref,
                       dsk_ref, nw_ref, exp_ref,
                       y_ref, h_ref, ht_scr, *, seq):
    ht_scr[...] = jnp.zeros_like(ht_scr)
    row = lax.broadcasted_iota(jnp.int32, (TILE, TILE), 0)
    col = lax.broadcasted_iota(jnp.int32, (TILE, TILE), 1)
    causal = row >= col
    head_of_lane = lax.broadcasted_iota(jnp.int32, (TILE, GROUP_W), 1) // SSM_HEADDIM
    expand = exp_ref[0]

    def tile_body(t, carry):
        r0 = pl.multiple_of(t * TILE, TILE)
        rows = pl.ds(r0, TILE)
        xc = _conv_silu_tile(x_ref, wx_ref, cbx_ref, r0, t)
        bc = _conv_silu_tile(bm_ref, wb_ref, cbb_ref, r0, t)
        cc = _conv_silu_tile(cm_ref, wc_ref, cbc_ref, r0, t)
        dtv = _dot_sel(dt_ref[rows, :], expand, terms=2)
        cum = _dot_sel(cum_ref[rows, :], expand)
        cum_end = cum[TILE - 1:TILE, :]
        cum_t = cum.T
        xdt = xc * dtv
        xdtb = xdt.astype(BF16)
        ccb = cc.astype(BF16)
        cb = jnp.where(causal, _dot_nt(ccb, bc.astype(BF16)), 0.0)
        y = _dot(ccb, ht_scr[...].astype(BF16)) * jnp.exp(cum)
        for k in range(SSM_HPG):
            c_col = cum[:, k * SSM_HEADDIM:k * SSM_HEADDIM + 1]
            c_row = cum_t[k * SSM_HEADDIM:k * SSM_HEADDIM + 1, :]
            decay = jnp.exp(jnp.minimum(c_col - c_row, 0.0))
            m_k = (cb * decay).astype(BF16)
            x_k = jnp.where(head_of_lane == k, xdtb, jnp.zeros_like(xdtb))
            y = y + _dot(m_k, x_k)
        xs = (xdt * jnp.exp(cum_end - cum)).astype(BF16)
        ht_scr[...] = ht_scr[...] * jnp.exp(cum_end) + _dot(bc.T.astype(BF16), xs)
        y = (y + dsk_ref[...] * xc) * _silu(z_ref[rows, :])
        y_ref[rows, :] = _rms(y, nw_ref[...]).astype(y_ref.dtype)
        return carry

    lax.fori_loop(0, seq // TILE, tile_body, 0, unroll=2)
    h_ref[0] = ht_scr[...].T.reshape(SSM_HPG, SSM_HEADDIM, SSM_STATE)


def _ssd_prompt(proj, dt, cum, p, l):
    seq = SEQ
    xbc = COL_XBC
    gspec = lambda width, c0: pl.BlockSpec((seq, width), lambda b, g: (b, c0 // width + g))
    wspec = lambda rows, width, c0: pl.BlockSpec(
        (1, rows, width), lambda b, g: (l, 0, c0 // width + g))
    pspec = pl.BlockSpec((1, 1, GROUP_W), lambda b, g: (l, 0, g))
    args = (proj, proj, proj, proj, dt, cum,
            p['conv_w'], p['conv_w'], p['conv_w'], p['conv_b'], p['conv_b'], p['conv_b'],
            p['d_skip_x'], p['ssm_norm'], p['expand_g'])
    in_specs = [
        gspec(GROUP_W, xbc), gspec(SSM_STATE, xbc + SSM_INNER),
        gspec(SSM_STATE, xbc + SSM_INNER + SSM_GROUPS * SSM_STATE), gspec(GROUP_W, COL_Z),
        pl.BlockSpec((seq, DT_PAD), lambda b, g: (b, 0)),
        pl.BlockSpec((seq, DT_PAD), lambda b, g: (b, 0)),
        wspec(SSM_CONV, GROUP_W, 0), wspec(SSM_CONV, SSM_STATE, SSM_INNER),
        wspec(SSM_CONV, SSM_STATE, SSM_INNER + SSM_GROUPS * SSM_STATE),
        wspec(1, GROUP_W, 0), wspec(1, SSM_STATE, SSM_INNER),
        wspec(1, SSM_STATE, SSM_INNER + SSM_GROUPS * SSM_STATE),
        pspec, pspec,
        pl.BlockSpec((1, DT_PAD, GROUP_W), lambda b, g: (g, 0, 0)),
    ]

    def body(*refs):
        ins = list(refs[:15])
        for i in range(6, 14):
            ins[i] = ins[i].at[0]
        _ssd_prompt_kernel(*ins, *refs[15:], seq=seq)

    return pl.pallas_call(
        body,
        grid=(BATCH, SSM_GROUPS),
        in_specs=in_specs,
        out_specs=[
            pl.BlockSpec((seq, GROUP_W), lambda b, g: (b, g)),
            pl.BlockSpec((1, SSM_HPG, SSM_HEADDIM, SSM_STATE), lambda b, g: (b, g, 0, 0)),
        ],
        out_shape=[
            jax.ShapeDtypeStruct((BATCH * seq, SSM_INNER), BF16),
            jax.ShapeDtypeStruct((BATCH, SSM_HEADS, SSM_HEADDIM, SSM_STATE), F32),
        ],
        scratch_shapes=[pltpu.VMEM((SSM_STATE, GROUP_W), F32)],
        compiler_params=_cparams("arbitrary", "arbitrary"),
        name="ssd_prompt",
    )(*args)


HG_STEP_NB = 8


SUBLANES = 8


def _column(mat_t, lane_mask):
    return jnp.sum(jnp.where(lane_mask, mat_t, 0.0), axis=1, keepdims=True)


def _row_group(ref, bidx, cols):
    start = pl.multiple_of((bidx // SUBLANES) * SUBLANES, SUBLANES)
    idx = (pl.ds(start, SUBLANES), cols)
    blk = ref[idx]
    is_row = lax.broadcasted_iota(jnp.int32, blk.shape, 0) == bidx % SUBLANES
    return idx, blk, is_row


def _get_row(ref, bidx, cols):
    _, blk, is_row = _row_group(ref, bidx, cols)
    return jnp.sum(jnp.where(is_row, blk, 0.0), axis=0, keepdims=True)


def _set_row(ref, bidx, cols, row):
    idx, blk, is_row = _row_group(ref, bidx, cols)
    ref[idx] = jnp.where(is_row, row, blk)


BF16_ROWS = 16


def _one_hot_rows(row, bidx):
    sub = lax.broadcasted_iota(jnp.int32, (DEC_BATCH, row.shape[1]), 0)
    return jnp.where(sub == bidx, row, 0.0).astype(BF16)


def _row_times(row, mat_nt=None, mat=None):
    lhs = jnp.broadcast_to(row.astype(BF16), (BF16_ROWS, row.shape[1]))
    out = _dot(lhs, mat) if mat is not None else _dot_nt(lhs, mat_nt)
    return out[0:1]


def _hgrn_step_kernel(q_ref, f_ref, v_ref, og_ref, lb_ref, nw_ref, s_ref, o_ref, so_ref,
                      dec_t, k_t, q_s, *, first, zero_rest):
    i = pl.program_id(0)

    @pl.when(i == 0)
    def _():
        g, k = _hgrn_gates(f_ref[...], lb_ref[0], first)
        dec = jnp.exp(g)
        q_s[...] = _silu(q_ref[...]) * (HG_DK ** -0.5)
        o_ref[...] = jnp.zeros_like(o_ref)
        for h in range(HG_HEADS):
            hs = slice(h * HG_DK, (h + 1) * HG_DK)
            dec_t[hs, :] = dec[:, hs].T
            k_t[hs, :] = k[:, hs].T.astype(BF16)

    if zero_rest:
        so_ref[1:] = jnp.zeros((DEPTH - 1,) + so_ref.shape[1:], F32)

    lane = lax.broadcasted_iota(jnp.int32, (HG_DK, DEC_BATCH), 1)

    def body(j, carry):
        bidx = i * HG_STEP_NB + j
        m = lane == bidx
        for h in range(HG_HEADS):
            hs = slice(h * HG_DK, (h + 1) * HG_DK)
            d_col = _column(dec_t[hs, :], m)
            kv = _dot(k_t[hs, :], _one_hot_rows(_get_row(v_ref, bidx, hs), bidx))
            s_new = d_col * s_ref[0, j, h] + kv
            so_ref[0, j, h] = s_new
            o_row = _row_times(_get_row(q_s, bidx, hs), mat=s_new.astype(BF16))
            _set_row(o_ref, bidx, hs, _rms(o_row, nw_ref[0]) * _silu(_get_row(og_ref, bidx, hs)))
        return carry

    lax.fori_loop(0, HG_STEP_NB, body, 0)


def _stacked_state_specs(state, nb, l, out_state):
    tail = state.shape[2:]
    zeros = (0,) * len(tail)
    in_spec = pl.BlockSpec((1, nb) + tail, lambda i: (l, i) + zeros)
    if out_state is None:
        return in_spec, pl.BlockSpec((DEPTH, nb) + tail, lambda i: (0, i) + zeros), [], [], {}
    out_spec = pl.BlockSpec((1, nb) + tail, lambda i: (l, i) + zeros)
    return in_spec, out_spec, [pl.BlockSpec(memory_space=pl.ANY)], [out_state], None


def _hgrn_step(proj, state, lbs, hg_norm, out_state, l):
    nb = HG_STEP_NB
    blk = lambda c0: pl.BlockSpec((DEC_BATCH, D_MODEL), lambda i: (0, c0 // D_MODEL))
    sspec, ospec, extra_specs, extra_args, aliases = _stacked_state_specs(state, nb, l, out_state)
    in_specs = [
        blk(COL_Q), blk(COL_F), blk(COL_I), blk(COL_OG),
        pl.BlockSpec((1, 1, D_MODEL), lambda i: (l, 0, 0)),
        pl.BlockSpec((1, 1, HG_DV), lambda i: (l, 0, 0)),
        sspec,
    ] + extra_specs
    args = [proj, proj, proj, proj, lbs, hg_norm, state] + extra_args
    if aliases is None:
        aliases = {len(args) - 1: 1}

    def body(*refs):
        ins, rest = refs[:7], refs[len(args):]
        _hgrn_step_kernel(*ins, *rest, first=(l == 0), zero_rest=(out_state is None))

    return pl.pallas_call(
        body,
        grid=(DEC_BATCH // nb,),
        in_specs=in_specs,
        out_specs=[pl.BlockSpec((DEC_BATCH, D_MODEL), lambda i: (0, 0)), ospec],
        out_shape=[
            jax.ShapeDtypeStruct((DEC_BATCH, D_MODEL), F32),
            jax.ShapeDtypeStruct(state.shape, F32),
        ],
        scratch_shapes=[pltpu.VMEM((D_MODEL, DEC_BATCH), F32), pltpu.VMEM((D_MODEL, DEC_BATCH), BF16),
                        pltpu.VMEM((DEC_BATCH, D_MODEL), F32)],
        input_output_aliases=aliases,
        compiler_params=_cparams("arbitrary"),
        name="hgrn_step",
    )(*args)


def _ssm_prep_kernel(xs_ref, bm_ref, cm_ref, dt_ref, cst_ref, w_ref, cb_ref,
                     alog_ref, exp_ref,
                     xc_ref, bc_ref, cc_ref, dec_ref, xdtt_ref, cnew_ref):
    ch = SSM_CONV_CH
    parts = ((xs_ref, 0, SSM_INNER, xc_ref),
             (bm_ref, SSM_INNER, SSM_INNER + 1024, bc_ref),
             (cm_ref, SSM_INNER + 1024, ch, cc_ref))
    for u_ref, c0, c1, out_ref in parts:
        u = u_ref[...]
        acc = cb_ref[:, c0:c1] + w_ref[SSM_CONV - 1:SSM_CONV, c0:c1] * u
        for j in range(SSM_CONV - 1):
            acc = acc + w_ref[j:j + 1, c0:c1] * cst_ref[:, j * ch + c0:j * ch + c1]
        out_ref[...] = _silu(acc)
        cnew_ref[:, (SSM_CONV - 2) * ch + c0:(SSM_CONV - 2) * ch + c1] = u
    cnew_ref[:, 0:(SSM_CONV - 2) * ch] = cst_ref[:, ch:(SSM_CONV - 1) * ch]
    dtv = _dot_sel(dt_ref[...], exp_ref[...])
    dec_ref[...] = jnp.exp(dtv * -jnp.exp(alog_ref[...]))
    xdt = xc_ref[...] * dtv
    for m in range(SSM_INNER // LANE):
        ms = slice(m * LANE, (m + 1) * LANE)
        xdtt_ref[ms, :] = xdt[:, ms].T.astype(BF16)


def _ssm_prep(proj, dt, conv_state, p, l):
    n = DEC_BATCH
    full = lambda shape: pl.BlockSpec(shape, lambda i: tuple(0 for _ in shape))
    lspec = lambda rows, width: pl.BlockSpec((1, rows, width), lambda i: (l, 0, 0))

    def body(xs, bm, cm, dtr, cst, w, cb, alog, ex, *outs):
        _ssm_prep_kernel(xs, bm, cm, dtr, cst.at[0], w.at[0], cb.at[0], alog.at[0], ex, *outs)

    return pl.pallas_call(
        body,
        grid=(1,),
        in_specs=[
            pl.BlockSpec((n, SSM_INNER), lambda i: (0, COL_XBC // SSM_INNER)),
            pl.BlockSpec((n, 1024), lambda i: (0, (COL_XBC + SSM_INNER) // 1024)),
            pl.BlockSpec((n, 1024), lambda i: (0, (COL_XBC + SSM_INNER) // 1024 + 1)),
            full((n, DT_PAD)),
            pl.BlockSpec((1, n, (SSM_CONV - 1) * SSM_CONV_CH), lambda i: (l, 0, 0)),
            lspec(SSM_CONV, SSM_CONV_CH), lspec(1, SSM_CONV_CH),
            lspec(1, SSM_INNER),
            full((DT_PAD, SSM_INNER)),
        ],
        out_specs=[
            full((n, SSM_INNER)), full((n, 1024)), full((n, 1024)),
            full((n, SSM_INNER)), full((SSM_INNER, n)),
            full((n, (SSM_CONV - 1) * SSM_CONV_CH)),
        ],
        out_shape=[
            jax.ShapeDtypeStruct((n, SSM_INNER), F32),
            jax.ShapeDtypeStruct((n, 1024), F32),
            jax.ShapeDtypeStruct((n, 1024), F32),
            jax.ShapeDtypeStruct((n, SSM_INNER), F32),
            jax.ShapeDtypeStruct((SSM_INNER, n), BF16),
            jax.ShapeDtypeStruct((n, (SSM_CONV - 1) * SSM_CONV_CH), F32),
        ],
        compiler_params=_cparams("arbitrary"),
        name="ssm_prep",
    )(proj, proj, proj, dt, conv_state, p['conv_w'], p['conv_b'], p['a_log_x'], p['expand_all'])


SSM_STEP_NB = 4


def _ssm_step_kernel(dec_ref, xdtt_ref, bc_ref, cc_ref, s_ref, y_ref, so_ref, *, zero_rest):
    i = pl.program_id(0)

    @pl.when(i == 0)
    def _():
        y_ref[...] = jnp.zeros_like(y_ref)

    if zero_rest:
        so_ref[1:] = jnp.zeros((DEPTH - 1,) + so_ref.shape[1:], F32)

    def body(j, carry):
        bidx = i * SSM_STEP_NB + j
        for g in range(SSM_GROUPS):
            rs = slice(g * GROUP_W, (g + 1) * GROUP_W)
            hs = slice(g * SSM_HPG, (g + 1) * SSM_HPG)
            ns = slice(g * SSM_STATE, (g + 1) * SSM_STATE)
            dec_row = _get_row(dec_ref, bidx, rs)
            decay = jnp.concatenate(
                [jnp.broadcast_to(dec_row[:, k * SSM_HEADDIM:k * SSM_HEADDIM + 1],
                                  (SSM_HEADDIM, SSM_STATE)) for k in range(SSM_HPG)], axis=0)
            inc = _dot(xdtt_ref[rs, :], _one_hot_rows(_get_row(bc_ref, bidx, ns), bidx))
            h = s_ref[0, j, hs].reshape(GROUP_W, SSM_STATE)
            h_new = decay * h + inc
            so_ref[0, j, hs] = h_new.reshape(SSM_HPG, SSM_HEADDIM, SSM_STATE)
            y_row = _row_times(_get_row(cc_ref, bidx, ns), mat_nt=h_new.astype(BF16))
            _set_row(y_ref, bidx, rs, y_row)
        return carry

    lax.fori_loop(0, SSM_STEP_NB, body, 0)


def _ssm_step(dec, xdtt, bc, cc, state, out_state, l):
    nb = SSM_STEP_NB
    full = lambda shape: pl.BlockSpec(shape, lambda i: tuple(0 for _ in shape))
    sspec, ospec, extra_specs, extra_args, aliases = _stacked_state_specs(state, nb, l, out_state)
    in_specs = [full(dec.shape), full(xdtt.shape), full(bc.shape), full(cc.shape),
                sspec] + extra_specs
    args = [dec, xdtt, bc, cc, state] + extra_args
    if aliases is None:
        aliases = {len(args) - 1: 1}

    def body(*refs):
        ins, rest = refs[:5], refs[len(args):]
        _ssm_step_kernel(*ins, *rest, zero_rest=(out_state is None))

    return pl.pallas_call(
        body,
        grid=(DEC_BATCH // nb,),
        in_specs=in_specs,
        out_specs=[full((DEC_BATCH, SSM_INNER)), ospec],
        out_shape=[
            jax.ShapeDtypeStruct((DEC_BATCH, SSM_INNER), F32),
            jax.ShapeDtypeStruct(state.shape, F32),
        ],
        input_output_aliases=aliases,
        compiler_params=_cparams("arbitrary"),
        name="ssm_step",
    )(*args)


def _ssm_post_kernel(y_in_ref, xc_ref, z_ref, dsk_ref, nw_ref, y_ref):
    for g in range(SSM_GROUPS):
        gs = slice(g * GROUP_W, (g + 1) * GROUP_W)
        y = (y_in_ref[:, gs] + dsk_ref[:, gs] * xc_ref[:, gs]) * _silu(z_ref[:, gs])
        y_ref[:, gs] = _rms(y, nw_ref[:, gs])


def _ssm_post(y, xc, proj, p, l):
    n = DEC_BATCH
    full = lambda shape: pl.BlockSpec(shape, lambda i: tuple(0 for _ in shape))
    lspec = pl.BlockSpec((1, 1, SSM_INNER), lambda i: (l, 0, 0))

    def body(y_in, xc_r, z_r, dsk, nw, y_r):
        _ssm_post_kernel(y_in, xc_r, z_r, dsk.at[0], nw.at[0], y_r)

    return pl.pallas_call(
        body,
        grid=(1,),
        in_specs=[full((n, SSM_INNER)), full((n, SSM_INNER)),
                  pl.BlockSpec((n, SSM_INNER), lambda i: (0, COL_Z // SSM_INNER)),
                  lspec, lspec],
        out_specs=full((n, SSM_INNER)),
        out_shape=jax.ShapeDtypeStruct((n, SSM_INNER), F32),
        compiler_params=_cparams("arbitrary"),
        name="ssm_post",
    )(y, xc, proj, p['d_skip_x'], p['ssm_norm'])


def _merge_kernel(x_ref, oh_ref, y_ref, ga_ref, gb_ref, g1_ref, bma_ref, bmb_ref,
                  wa_ref, wb_ref, wo_ref, o_ref):
    a = _dot(oh_ref[...].astype(BF16), wa_ref[0])
    b = _dot(y_ref[...].astype(BF16), wb_ref[0])
    u = _sigmoid(ga_ref[...] + bma_ref[0]) * a + _sigmoid(gb_ref[...] + bmb_ref[0]) * b
    o_ref[...] = x_ref[...] + g1_ref[0] * _dot(u.astype(BF16), wo_ref[0])


def _merge(x, o_hg, y, proj, mod, p, l, tm, rows_per_group):
    rows = x.shape[0]
    wspec = lambda k: pl.BlockSpec((1, k, D_MODEL), lambda i: (l, 0, 0))
    return pl.pallas_call(
        _merge_kernel,
        grid=(rows // tm,),
        in_specs=[
            pl.BlockSpec((tm, D_MODEL), lambda i: (i, 0)),
            pl.BlockSpec((tm, D_MODEL), lambda i: (i, 0)),
            pl.BlockSpec((tm, SSM_INNER), lambda i: (i, 0)),
            pl.BlockSpec((tm, D_MODEL), lambda i: (i, COL_GA // D_MODEL)),
            pl.BlockSpec((tm, D_MODEL), lambda i: (i, COL_GB // D_MODEL)),
            _mod_spec(mod, tm, rows_per_group, ADA_G1),
            pl.BlockSpec((1, 1, D_MODEL), lambda i: (l, 0, 0)),
            pl.BlockSpec((1, 1, D_MODEL), lambda i: (l, 0, 1)),
            wspec(D_MODEL), wspec(SSM_INNER), wspec(D_MODEL),
        ],
        out_specs=pl.BlockSpec((tm, D_MODEL), lambda i: (i, 0)),
        out_shape=jax.ShapeDtypeStruct((rows, D_MODEL), F32),
        compiler_params=_cparams("arbitrary"),
        name="merge",
    )(x, o_hg, y, proj, proj, mod, p['b_merge'], p['b_merge'], p['w_br_a'], p['w_br_b'],
      p['w_out'])


FF_CHUNK = 1024


def _mlp_kernel(x_ref, sc_ref, sh_ref, g_ref, nw_ref, wu_ref, wd_ref, fw_ref, o_ref, *, final):
    x = x_ref[...]
    h = (_rms(x, nw_ref[0]) * (1.0 + sc_ref[0]) + sh_ref[0]).astype(BF16)
    acc = jnp.zeros(x.shape, F32)
    for c in range(D_FF // FF_CHUNK):
        cs = slice(c * FF_CHUNK, (c + 1) * FF_CHUNK)
        a = jnp.maximum(_dot(h, wu_ref[0, :, cs]), 0.0)
        acc = acc + _dot((a * a).astype(BF16), wd_ref[0, cs, :])
    out = x + g_ref[0] * acc
    if final:
        out = _rms(out, fw_ref[...])
    o_ref[...] = out


def _mlp(x, mod, p, norm_final, l, tm, rows_per_group):
    rows = x.shape[0]
    return pl.pallas_call(
        functools.partial(_mlp_kernel, final=(l == DEPTH - 1)),
        grid=(rows // tm,),
        in_specs=[
            pl.BlockSpec((tm, D_MODEL), lambda i: (i, 0)),
            _mod_spec(mod, tm, rows_per_group, ADA_SC2),
            _mod_spec(mod, tm, rows_per_group, ADA_SH2),
            _mod_spec(mod, tm, rows_per_group, ADA_G2),
            pl.BlockSpec((1, 1, D_MODEL), lambda i: (l, 0, 0)),
            pl.BlockSpec((1, D_MODEL, D_FF), lambda i: (l, 0, 0)),
            pl.BlockSpec((1, D_FF, D_MODEL), lambda i: (l, 0, 0)),
            pl.BlockSpec((1, D_MODEL), lambda i: (0, 0)),
        ],
        out_specs=pl.BlockSpec((tm, D_MODEL), lambda i: (i, 0)),
        out_shape=jax.ShapeDtypeStruct((rows, D_MODEL), F32),
        compiler_params=_cparams("arbitrary"),
        name="mlp",
    )(x, mod, mod, mod, p['norm_mlp'], p['w_up'], p['w_down'], norm_final)


def _prepare_params(w_ada, b_ada, norm_mix, w_in, b_merge, hg_norm, conv_w, conv_b, dt_bias,
                    a_log, d_skip, ssm_norm, w_br_a, w_br_b, w_out, norm_mlp, w_up, w_down):
    dt0 = COL_XBC + SSM_CONV_CH
    w_main = jnp.concatenate([w_in[:, :, :dt0], w_in[:, :, dt0 + SSM_HEADS:]], axis=-1)
    w_dt = jnp.pad(w_in[:, :, dt0:dt0 + SSM_HEADS], ((0, 0), (0, 0), (0, DT_PAD - SSM_HEADS)))
    per_head = lambda a: jnp.repeat(a, SSM_HEADDIM, axis=-1).reshape(DEPTH, 1, SSM_INNER)
    narrow = lambda a: jnp.pad(a, ((0, 0), (0, DT_PAD - SSM_HEADS))).reshape(DEPTH, 1, DT_PAD)
    head_of_col = jnp.arange(SSM_INNER) // SSM_HEADDIM
    expand_all = (jnp.arange(DT_PAD)[:, None] == head_of_col[None, :]).astype(BF16)
    expand_g = expand_all.reshape(DT_PAD, SSM_GROUPS, GROUP_W).transpose(1, 0, 2)
    return {
        'norm_mix': norm_mix.reshape(DEPTH, 1, D_MODEL),
        'w_main': w_main.astype(BF16), 'w_dt': w_dt.astype(BF16),
        'b_merge': b_merge.reshape(DEPTH, 1, 2 * D_MODEL),
        'hg_norm': hg_norm.reshape(DEPTH, 1, HG_DV),
        'conv_w': conv_w, 'conv_b': conv_b.reshape(DEPTH, 1, SSM_CONV_CH),
        'dt_bias_n': narrow(dt_bias), 'a_log_n': narrow(a_log),
        'a_log_x': per_head(a_log), 'd_skip_x': per_head(d_skip),
        'ssm_norm': ssm_norm.reshape(DEPTH, 1, SSM_INNER),
        'expand_all': expand_all, 'expand_g': expand_g,
        'w_br_a': w_br_a.astype(BF16), 'w_br_b': w_br_b.astype(BF16),
        'w_out': w_out.astype(BF16),
        'norm_mlp': norm_mlp.reshape(DEPTH, 1, D_MODEL),
        'w_up': w_up.astype(BF16), 'w_down': w_down.astype(BF16),
    }


def kernel(x_prompt, x_sample, state_hgrn, state_ssm, state_conv, c_prompt, c_sample, w_ada, b_ada, norm_mix, w_in, b_merge, lower_bounds, hg_norm, conv_w, conv_b, dt_bias, a_log, d_skip, ssm_norm, w_br_a, w_br_b, w_out, norm_mlp, w_up, w_down, norm_final):
    p = _prepare_params(w_ada, b_ada, norm_mix, w_in, b_merge, hg_norm, conv_w, conv_b, dt_bias,
                        a_log, d_skip, ssm_norm, w_br_a, w_br_b, w_out, norm_mlp, w_up, w_down)
    lbs = _lower_bounds(lower_bounds).reshape(DEPTH, 1, HG_HEADS * HG_DK)
    mod = _modulation(jnp.concatenate([c_prompt, c_sample], axis=0), w_ada, b_ada)
    mod_p = mod[:, :BATCH].reshape(DEPTH, BATCH, 1, N_ADA * D_MODEL)
    mod_s = mod[:, BATCH:].reshape(DEPTH, 1, DEC_BATCH, N_ADA * D_MODEL)
    norm_final2 = norm_final.reshape(1, D_MODEL)
    conv_state = state_conv.reshape(DEPTH, DEC_BATCH, (SSM_CONV - 1) * SSM_CONV_CH)

    xp = x_prompt.reshape(BATCH * SEQ, D_MODEL)
    xs = x_sample.reshape(DEC_BATCH, D_MODEL)
    hg_p, ssm_p, cv_p, cv_s = [], [], [], []
    hg_s = ssm_s = None
    for l in range(DEPTH):
        proj, dt, cum = _inproj(xp, mod_p[l], p, l, SEQ, SEQ)
        o_hg, s_new = _hgrn_prompt(proj, lbs, p['hg_norm'], l)
        y, h_new = _ssd_prompt(proj, dt, cum, p, l)
        hg_p.append(s_new)
        ssm_p.append(h_new)
        cv_p.append(proj.reshape(BATCH, SEQ, PROJ_W)[:, SEQ - (SSM_CONV - 1):,
                                                       COL_XBC:COL_XBC + SSM_CONV_CH])
        xp = _merge(xp, o_hg, y, proj, mod_p[l], p, l, 512, SEQ)
        xp = _mlp(xp, mod_p[l], p, norm_final2, l, 512, SEQ)
        proj, dt, _ = _inproj(xs, mod_s[l], p, l, DEC_BATCH, DEC_BATCH)
        o_hg, hg_s = _hgrn_step(proj, state_hgrn, lbs, p['hg_norm'], hg_s, l)
        xc, bc, cc, dect, xdtt, cnew = _ssm_prep(proj, dt, conv_state, p, l)
        yt, ssm_s = _ssm_step(dect, xdtt, bc, cc, state_ssm, ssm_s, l)
        y = _ssm_post(yt, xc, proj, p, l)
        cv_s.append(cnew.reshape(DEC_BATCH, SSM_CONV - 1, SSM_CONV_CH))
        xs = _merge(xs, o_hg, y, proj, mod_s[l], p, l, DEC_BATCH, DEC_BATCH)
        xs = _mlp(xs, mod_s[l], p, norm_final2, l, DEC_BATCH, DEC_BATCH)
    return (xp.reshape(BATCH, SEQ, D_MODEL), xs.reshape(DEC_BATCH, 1, D_MODEL),
            jnp.stack(hg_p), jnp.stack(ssm_p), jnp.stack(cv_p),
            hg_s, ssm_s, jnp.stack(cv_s))
```

```python
import functools

import jax
import jax.numpy as jnp
import numpy as np
from jax import lax
from jax.experimental import pallas as pl
from jax.experimental.pallas import tpu as pltpu

F32 = jnp.float32
BF16 = jnp.bfloat16

D_MODEL = 1024
BATCH = 8
SEQ = 2048
DEPTH = 4
DEC_BATCH = 128
HG_HEADS = 8
HG_DK = 128
HG_DV = 128
SSM_INNER = 2048
SSM_HEADDIM = 64
SSM_HEADS = 32
SSM_GROUPS = 8
SSM_HPG = 4
SSM_STATE = 128
SSM_CONV = 4
SSM_CONV_CH = 4096
GROUP_W = SSM_HPG * SSM_HEADDIM
D_FF = 4096
N_ADA = 6
EPS = 1e-6
DT_PAD = 128
COL_Q, COL_F, COL_I, COL_OG = 0, 1024, 2048, 3072
COL_Z, COL_XBC, COL_GA, COL_GB = 4096, 6144, 10240, 11264
PROJ_W = 12288
ADA_SH1, ADA_SC1, ADA_G1, ADA_SH2, ADA_SC2, ADA_G2 = range(6)

LANE = 128
SUBLANES = 8
TILE = 128
VMEM_LIMIT = 56 * 1024 * 1024


def _cparams(*sem):
    return pltpu.CompilerParams(dimension_semantics=sem, vmem_limit_bytes=VMEM_LIMIT)


def _sigmoid(x):
    return 0.5 * jnp.tanh(0.5 * x) + 0.5


def _silu(x):
    return x * _sigmoid(x)


def _softplus(x):
    return jnp.maximum(x, 0.0) + jnp.log1p(jnp.exp(-jnp.abs(x)))


def _dot(a, b):
    return jnp.dot(a, b, preferred_element_type=F32)


def _dot_nt(a, b):
    return lax.dot_general(a, b, (((1,), (1,)), ((), ())), preferred_element_type=F32)


def _split(a, terms):
    parts = []
    for _ in range(terms - 1):
        hi = a.astype(BF16)
        parts.append(hi)
        a = a - hi.astype(F32)
    parts.append(a.astype(BF16))
    return parts


def _sel_dot(sel, a, terms=3):
    return _dot(jnp.concatenate([sel] * terms, axis=1), jnp.concatenate(_split(a, terms), axis=0))


def _dot_sel(a, sel, terms=3):
    return _dot(jnp.concatenate(_split(a, terms), axis=1), jnp.concatenate([sel] * terms, axis=0))


def _rms(x, w):
    ms = jnp.mean(x * x, axis=-1, keepdims=True)
    return x * lax.rsqrt(ms + EPS) * w


def _hgrn_gates(f, lb, first):
    if first:
        g = jnp.minimum(f, 0.0) - jnp.log1p(jnp.exp(-jnp.abs(f)))
        k = _sigmoid(-f)
    else:
        fg = lb + (1.0 - lb) * _sigmoid(f)
        g = jnp.log(fg)
        k = 1.0 - fg
    return g, k


def _lbs_kernel(lb_ref, o_ref):
    x = lb_ref[...]
    m = jnp.max(x, axis=0, keepdims=True)
    e = jnp.exp(x - m)
    p = e / jnp.sum(e, axis=0, keepdims=True)
    acc = jnp.zeros_like(p[0:1])
    rows = [acc]
    for l in range(1, DEPTH):
        acc = acc + p[l:l + 1]
        rows.append(acc)
    o_ref[...] = jnp.concatenate(rows, axis=0)


def _lower_bounds(lower_bounds):
    return pl.pallas_call(
        _lbs_kernel, out_shape=jax.ShapeDtypeStruct(lower_bounds.shape, F32), name="lbs",
    )(lower_bounds)


def _mod_kernel(c_ref, w_ref, b_ref, o_ref):
    a = _silu(c_ref[...]).astype(BF16)
    o_ref[0] = _dot(a, w_ref[0].astype(BF16)) + b_ref[0]


def _modulation(c_all, w_ada, b_ada):
    n = c_all.shape[0]
    tn = 1024
    return pl.pallas_call(
        _mod_kernel,
        grid=(DEPTH, N_ADA * D_MODEL // tn),
        in_specs=[
            pl.BlockSpec((n, D_MODEL), lambda l, j: (0, 0)),
            pl.BlockSpec((1, D_MODEL, tn), lambda l, j: (l, 0, j)),
            pl.BlockSpec((1, 1, tn), lambda l, j: (l, 0, j)),
        ],
        out_specs=pl.BlockSpec((1, n, tn), lambda l, j: (l, 0, j)),
        out_shape=jax.ShapeDtypeStruct((DEPTH, n, N_ADA * D_MODEL), F32),
        compiler_params=_cparams("arbitrary", "arbitrary"),
        name="modulation",
    )(c_all, w_ada, b_ada.reshape(DEPTH, 1, N_ADA * D_MODEL))


def _mod_spec(mod, tm, rows_per_group, piece):
    if mod.shape[1] == 1:
        tiles = rows_per_group // tm
        return pl.BlockSpec((1, 1, D_MODEL), lambda i, *_: (i // tiles, 0, piece))
    return pl.BlockSpec((1, tm, D_MODEL), lambda i, *_: (0, i, piece))


def _inproj_kernel(x_ref, sc_ref, sh_ref, nw_ref, w_ref, wdt_ref, dtb_ref, alog_ref,
                   o_ref, odt_ref, ocum_ref, h_scr):
    @pl.when(pl.program_id(1) == 0)
    def _():
        h = _rms(x_ref[...], nw_ref[0]) * (1.0 + sc_ref[0]) + sh_ref[0]
        hb = h.astype(BF16)
        h_scr[...] = hb
        dtv = _softplus(_dot(hb, wdt_ref[0]) + dtb_ref[0])
        odt_ref[...] = dtv
        la = dtv * -jnp.exp(alog_ref[0])
        row = lax.broadcasted_iota(jnp.int32, (TILE, TILE), 0)
        col = lax.broadcasted_iota(jnp.int32, (TILE, TILE), 1)
        tri = jnp.where(row >= col, 1.0, 0.0).astype(BF16)
        for i in range(la.shape[0] // TILE):
            rs = slice(i * TILE, (i + 1) * TILE)
            ocum_ref[rs, :] = _sel_dot(tri, la[rs] * LOG2E)

    o_ref[...] = _dot(h_scr[...], w_ref[0])


def _inproj(x, mod, p, l, tm, rows_per_group):
    rows = x.shape[0]
    tn = 1024
    narrow = pl.BlockSpec((1, 1, DT_PAD), lambda i, j: (l, 0, 0))
    return pl.pallas_call(
        _inproj_kernel,
        grid=(rows // tm, PROJ_W // tn),
        in_specs=[
            pl.BlockSpec((tm, D_MODEL), lambda i, j: (i, 0)),
            _mod_spec(mod, tm, rows_per_group, ADA_SC1),
            _mod_spec(mod, tm, rows_per_group, ADA_SH1),
            pl.BlockSpec((1, 1, D_MODEL), lambda i, j: (l, 0, 0)),
            pl.BlockSpec((1, D_MODEL, tn), lambda i, j: (l, 0, j)),
            pl.BlockSpec((1, D_MODEL, DT_PAD), lambda i, j: (l, 0, 0)),
            narrow, narrow,
        ],
        out_specs=[
            pl.BlockSpec((tm, tn), lambda i, j: (i, j)),
            pl.BlockSpec((tm, DT_PAD), lambda i, j: (i, 0)),
            pl.BlockSpec((tm, DT_PAD), lambda i, j: (i, 0)),
        ],
        out_shape=[
            jax.ShapeDtypeStruct((rows, PROJ_W), F32),
            jax.ShapeDtypeStruct((rows, DT_PAD), F32),
            jax.ShapeDtypeStruct((rows, DT_PAD), F32),
        ],
        scratch_shapes=[pltpu.VMEM((tm, D_MODEL), BF16)],
        compiler_params=_cparams("arbitrary", "arbitrary"),
        name="inproj",
    )(x, mod, mod, p['norm_mix'], p['w_main'], p['w_dt'], p['dt_bias_n'], p['a_log_n'])


HG_LEVELS = 7
HG_HB = 4
HG_SPLIT = 2
LOG2E = 1.4426950408889634


def _hgrn_sum_ranges():
    r = np.arange(TILE)[:, None]
    c = np.arange(TILE)[None, :]
    mats = [c <= r, c > r]
    for s in range(HG_LEVELS):
        h = 1 << s
        right = ((r >> s) & 1) == 1
        in_right = (c >= (r & ~(h - 1))) & (c <= r)
        in_left = (c > r) & (c <= (r | (h - 1)))
        mats.append(np.where(right, in_right, in_left))
    m = np.concatenate(mats, axis=0).astype(np.float32)
    return np.tile(m, (1, HG_SPLIT))


def _hgrn_level_masks():
    r = np.arange(TILE)[:, None]
    c = np.arange(TILE)[None, :]
    x = r ^ c
    top = np.where(x > 0, np.floor(np.log2(np.maximum(x, 1))).astype(np.int64), -1)
    masks = [r == c] + [(r > c) & (top == s) for s in range(HG_LEVELS)]
    return np.stack(masks).astype(np.float32)


def _hgrn_tile(rows, q_ref, f_ref, v_ref, og_ref, lb, nw, sums_ref, mask_ref, o_ref, st_scr, first):
    g, k = _hgrn_gates(f_ref[rows, :], lb, first)
    g2 = jnp.concatenate(_split(g * LOG2E, HG_SPLIT), axis=0)
    decay = jnp.exp2(_dot(sums_ref[...], g2))
    q = _silu(q_ref[rows, :]) * (HG_DK ** -0.5)
    v = v_ref[rows, :]
    og = _silu(og_ref[rows, :])
    q_in = (q * decay[0:TILE]).astype(BF16)
    k_out = (k * decay[TILE:2 * TILE]).astype(BF16)
    end = decay[TILE - 1:TILE]
    qb = q.astype(BF16)
    kb = k.astype(BF16)
    vb = v.astype(BF16)
    outs = []
    for hh in range(HG_HB):
        cs = slice(hh * HG_DK, (hh + 1) * HG_DK)
        st = st_scr[hh]
        p = mask_ref[0] * _dot_nt(qb[:, cs], kb[:, cs])
        for s in range(HG_LEVELS):
            a = decay[(2 + s) * TILE:(3 + s) * TILE, cs].astype(BF16)
            p = p + mask_ref[1 + s] * _dot_nt(qb[:, cs] * a, kb[:, cs] * a)
        o = _dot_nt(q_in[:, cs], st.astype(BF16)) + _dot(p.astype(BF16), vb[:, cs])
        st_scr[hh] = st * end[:, cs] + _dot(v[:, cs].T.astype(BF16), k_out[:, cs])
        outs.append(_rms(o, nw) * og[:, cs])
    o_ref[rows, :] = jnp.concatenate(outs, axis=1).astype(o_ref.dtype)


def _conv_silu_tile(ref, cols, w_ref, b_ref, tail, r0, t):
    cur = ref[pl.ds(r0, TILE), cols]
    prev = ref[pl.ds(pl.multiple_of(jnp.maximum(r0 - SUBLANES, 0), SUBLANES), SUBLANES), cols]
    prev = jnp.where(t > 0, prev, tail)
    ext = jnp.concatenate([prev, cur], axis=0)
    acc = b_ref[:, cols] + w_ref[SSM_CONV - 1:SSM_CONV, cols] * cur
    for j in range(SSM_CONV - 1):
        acc = acc + w_ref[j:j + 1, cols] * ext[5 + j:5 + j + TILE, :]
    return _silu(acc)


def _ssd_tile(rows, r0, t, gi, x_ref, bm_ref, cm_ref, z_ref, dt_ref, cum_ref,
              wx_ref, wb_ref, wc_ref, cbx_ref, cbb_ref, cbc_ref, dsk_ref, nw_ref, exp_ref,
              tail_scr, y_ref, ht_scr, causal, head_of_lane):
    xs = slice(gi * GROUP_W, (gi + 1) * GROUP_W)
    ns = slice(gi * SSM_STATE, (gi + 1) * SSM_STATE)
    b_off = MIX_GROUPS * GROUP_W
    c_off = b_off + MIX_GROUPS * SSM_STATE
    tail_b = slice(b_off + gi * SSM_STATE, b_off + (gi + 1) * SSM_STATE)
    tail_c = slice(c_off + gi * SSM_STATE, c_off + (gi + 1) * SSM_STATE)
    xc = _conv_silu_tile(x_ref, xs, wx_ref, cbx_ref, tail_scr[:, xs], r0, t)
    bc = _conv_silu_tile(bm_ref, ns, wb_ref, cbb_ref, tail_scr[:, tail_b], r0, t)
    cc = _conv_silu_tile(cm_ref, ns, wc_ref, cbc_ref, tail_scr[:, tail_c], r0, t)
    expand = exp_ref[gi]
    dtv = _dot_sel(dt_ref[rows, :], expand, terms=2)
    cum = _dot_sel(cum_ref[rows, :], expand)
    cum_end = cum[TILE - 1:TILE, :]
    cum_t = cum.T
    xdt = xc * dtv
    xdtb = xdt.astype(BF16)
    ccb = cc.astype(BF16)
    cb = jnp.where(causal, _dot_nt(ccb, bc.astype(BF16)), 0.0)
    ht = ht_scr[gi]
    weights, values = [], []
    for k in range(SSM_HPG):
        c_col = cum[:, k * SSM_HEADDIM:k * SSM_HEADDIM + 1]
        c_row = cum_t[k * SSM_HEADDIM:k * SSM_HEADDIM + 1, :]
        decay = jnp.exp2(jnp.minimum(c_col - c_row, 0.0))
        weights.append((cb * decay).astype(BF16))
        values.append(jnp.where(head_of_lane == k, xdtb, jnp.zeros_like(xdtb)))
    y = (_dot(ccb, ht.astype(BF16)) * jnp.exp2(cum)
         + _dot(jnp.concatenate(weights, axis=1), jnp.concatenate(values, axis=0)))
    x_out = (xdt * jnp.exp2(cum_end - cum)).astype(BF16)
    ht_scr[gi] = ht * jnp.exp2(cum_end) + _dot(bc.T.astype(BF16), x_out)
    y = (y + dsk_ref[:, xs] * xc) * _silu(z_ref[rows, xs])
    y_ref[rows, xs] = _rms(y, nw_ref[:, xs]).astype(y_ref.dtype)


MIX_ROWS = 512
MIX_GROUPS = SSM_GROUPS // (HG_HEADS // HG_HB)


def _mixer_prompt_kernel(q_ref, f_ref, v_ref, og_ref, lb_ref, hnw_ref, sums_ref, mask_ref,
                         x_ref, bm_ref, cm_ref, z_ref, dt_ref, cum_ref,
                         wx_ref, wb_ref, wc_ref, cbx_ref, cbb_ref, cbc_ref,
                         dsk_ref, snw_ref, exp_ref,
                         o_ref, s_ref, y_ref, h_ref, st_scr, ht_scr, tail_scr, *, first):
    chunk = pl.program_id(2)

    @pl.when(chunk == 0)
    def _():
        st_scr[...] = jnp.zeros_like(st_scr)
        ht_scr[...] = jnp.zeros_like(ht_scr)
        tail_scr[...] = jnp.zeros_like(tail_scr)

    row = lax.broadcasted_iota(jnp.int32, (TILE, TILE), 0)
    col = lax.broadcasted_iota(jnp.int32, (TILE, TILE), 1)
    causal = row >= col
    head_of_lane = lax.broadcasted_iota(jnp.int32, (TILE, GROUP_W), 1) // SSM_HEADDIM
    lb = lb_ref[0]
    hnw = hnw_ref[0]

    def tile_body(t, carry):
        r0 = pl.multiple_of(t * TILE, TILE)
        rows = pl.ds(r0, TILE)
        _hgrn_tile(rows, q_ref, f_ref, v_ref, og_ref, lb, hnw, sums_ref, mask_ref, o_ref, st_scr,
                   first)
        for gi in range(MIX_GROUPS):
            _ssd_tile(rows, r0, t, gi, x_ref, bm_ref, cm_ref, z_ref, dt_ref, cum_ref,
                      wx_ref, wb_ref, wc_ref, cbx_ref, cbb_ref, cbc_ref, dsk_ref, snw_ref, exp_ref,
                      tail_scr, y_ref, ht_scr, causal, head_of_lane)
        return carry

    lax.fori_loop(0, MIX_ROWS // TILE, tile_body, 0, unroll=2)

    last = slice(MIX_ROWS - SUBLANES, MIX_ROWS)
    b_off = MIX_GROUPS * GROUP_W
    c_off = b_off + MIX_GROUPS * SSM_STATE
    tail_scr[:, 0:b_off] = x_ref[last, :]
    tail_scr[:, b_off:c_off] = bm_ref[last, :]
    tail_scr[:, c_off:] = cm_ref[last, :]

    @pl.when(chunk == pl.num_programs(2) - 1)
    def _():
        for hh in range(HG_HB):
            s_ref[0, hh] = st_scr[hh].T
        for gi in range(MIX_GROUPS):
            h_ref[0, gi * SSM_HPG:(gi + 1) * SSM_HPG] = ht_scr[gi].T.reshape(
                SSM_HPG, SSM_HEADDIM, SSM_STATE)


def _mixer_prompt(proj, dt, cum, lbs, p, l):
    rows = MIX_ROWS
    n_chunks = SEQ // rows
    hw = HG_HB * HG_DK
    xw = MIX_GROUPS * GROUP_W
    nw = MIX_GROUPS * SSM_STATE
    b0 = COL_XBC + SSM_INNER
    c0 = b0 + SSM_GROUPS * SSM_STATE
    act = lambda width, col0: pl.BlockSpec(
        (rows, width), lambda b, h, c: (b * n_chunks + c, col0 // width + h))
    par = lambda nrows, width, col0: pl.BlockSpec(
        (1, nrows, width), lambda b, h, c: (l, 0, col0 // width + h))
    narrow = pl.BlockSpec((rows, DT_PAD), lambda b, h, c: (b * n_chunks + c, 0))
    sums = jnp.asarray(_hgrn_sum_ranges(), BF16)
    masks = jnp.asarray(_hgrn_level_masks(), F32)
    in_specs = [
        act(hw, COL_Q), act(hw, COL_F), act(hw, COL_I), act(hw, COL_OG),
        par(1, hw, 0), pl.BlockSpec((1, 1, HG_DV), lambda b, h, c: (l, 0, 0)),
        pl.BlockSpec(sums.shape, lambda b, h, c: (0, 0)),
        pl.BlockSpec(masks.shape, lambda b, h, c: (0, 0, 0)),
        act(xw, COL_XBC), act(nw, b0), act(nw, c0), act(xw, COL_Z), narrow, narrow,
        par(SSM_CONV, xw, 0), par(SSM_CONV, nw, SSM_INNER),
        par(SSM_CONV, nw, SSM_INNER + SSM_GROUPS * SSM_STATE),
        par(1, xw, 0), par(1, nw, SSM_INNER), par(1, nw, SSM_INNER + SSM_GROUPS * SSM_STATE),
        par(1, xw, 0), par(1, xw, 0),
        pl.BlockSpec((MIX_GROUPS, DT_PAD, GROUP_W), lambda b, h, c: (h, 0, 0)),
    ]
    args = (proj, proj, proj, proj, lbs, p['hg_norm'], sums, masks,
            proj, proj, proj, proj, dt, cum,
            p['conv_w'], p['conv_w'], p['conv_w'], p['conv_b'], p['conv_b'], p['conv_b'],
            p['d_skip_x'], p['ssm_norm'], p['expand_g'])

    def body(*refs):
        ins = list(refs[:len(args)])
        for i in range(14, 22):
            ins[i] = ins[i].at[0]
        _mixer_prompt_kernel(*ins, *refs[len(args):], first=(l == 0))

    return pl.pallas_call(
        body,
        grid=(BATCH, HG_HEADS // HG_HB, n_chunks),
        in_specs=in_specs,
        out_specs=[
            pl.BlockSpec((rows, hw), lambda b, h, c: (b * n_chunks + c, h)),
            pl.BlockSpec((1, HG_HB, HG_DK, HG_DV), lambda b, h, c: (b, h, 0, 0)),
            pl.BlockSpec((rows, xw), lambda b, h, c: (b * n_chunks + c, h)),
            pl.BlockSpec((1, MIX_GROUPS * SSM_HPG, SSM_HEADDIM, SSM_STATE),
                         lambda b, h, c: (b, h, 0, 0)),
        ],
        out_shape=[
            jax.ShapeDtypeStruct((BATCH * SEQ, HG_HEADS * HG_DV), BF16),
            jax.ShapeDtypeStruct((BATCH, HG_HEADS, HG_DK, HG_DV), F32),
            jax.ShapeDtypeStruct((BATCH * SEQ, SSM_INNER), BF16),
            jax.ShapeDtypeStruct((BATCH, SSM_HEADS, SSM_HEADDIM, SSM_STATE), F32),
        ],
        scratch_shapes=[
            pltpu.VMEM((HG_HB, HG_DV, HG_DK), F32),
            pltpu.VMEM((MIX_GROUPS, SSM_STATE, GROUP_W), F32),
            pltpu.VMEM((SUBLANES, xw + 2 * nw), F32),
        ],
        compiler_params=_cparams("arbitrary", "arbitrary", "arbitrary"),
        name="mixer_prompt",
    )(*args)


HG_STEP_NB = 8


def _column(mat_t, lane_mask):
    return jnp.sum(jnp.where(lane_mask, mat_t, 0.0), axis=1, keepdims=True)


def _row_group(ref, bidx, cols):
    start = pl.multiple_of((bidx // SUBLANES) * SUBLANES, SUBLANES)
    idx = (pl.ds(start, SUBLANES), cols)
    blk = ref[idx]
    is_row = lax.broadcasted_iota(jnp.int32, blk.shape, 0) == bidx % SUBLANES
    return idx, blk, is_row


def _get_row(ref, bidx, cols):
    _, blk, is_row = _row_group(ref, bidx, cols)
    return jnp.sum(jnp.where(is_row, blk, 0.0), axis=0, keepdims=True)


def _set_row(ref, bidx, cols, row):
    idx, blk, is_row = _row_group(ref, bidx, cols)
    ref[idx] = jnp.where(is_row, row, blk)


BF16_ROWS = 16


def _one_hot_rows(row, bidx):
    sub = lax.broadcasted_iota(jnp.int32, (DEC_BATCH, row.shape[1]), 0)
    return jnp.where(sub == bidx, row, 0.0).astype(BF16)


def _row_times(row, mat_nt=None, mat=None):
    lhs = jnp.broadcast_to(row.astype(BF16), (BF16_ROWS, row.shape[1]))
    out = _dot(lhs, mat) if mat is not None else _dot_nt(lhs, mat_nt)
    return out[0:1]


def _hgrn_step_kernel(q_ref, f_ref, v_ref, og_ref, lb_ref, nw_ref, s_ref, o_ref, so_ref,
                      dec_t, k_t, q_s, *, first, zero_rest):
    i = pl.program_id(0)

    @pl.when(i == 0)
    def _():
        g, k = _hgrn_gates(f_ref[...], lb_ref[0], first)
        dec = jnp.exp(g)
        q_s[...] = _silu(q_ref[...]) * (HG_DK ** -0.5)
        o_ref[...] = jnp.zeros_like(o_ref)
        for h in range(HG_HEADS):
            hs = slice(h * HG_DK, (h + 1) * HG_DK)
            dec_t[hs, :] = dec[:, hs].T
            k_t[hs, :] = k[:, hs].T.astype(BF16)

    if zero_rest:
        so_ref[1:] = jnp.zeros((DEPTH - 1,) + so_ref.shape[1:], F32)

    lane = lax.broadcasted_iota(jnp.int32, (HG_DK, DEC_BATCH), 1)

    def body(j, carry):
        bidx = i * HG_STEP_NB + j
        m = lane == bidx
        for h in range(HG_HEADS):
            hs = slice(h * HG_DK, (h + 1) * HG_DK)
            d_col = _column(dec_t[hs, :], m)
            kv = _dot(k_t[hs, :], _one_hot_rows(_get_row(v_ref, bidx, hs), bidx))
            s_new = d_col * s_ref[0, j, h] + kv
            so_ref[0, j, h] = s_new
            o_row = _row_times(_get_row(q_s, bidx, hs), mat=s_new.astype(BF16))
            _set_row(o_ref, bidx, hs, _rms(o_row, nw_ref[0]) * _silu(_get_row(og_ref, bidx, hs)))
        return carry

    lax.fori_loop(0, HG_STEP_NB, body, 0)


def _stacked_state_specs(state, nb, l, out_state):
    tail = state.shape[2:]
    zeros = (0,) * len(tail)
    in_spec = pl.BlockSpec((1, nb) + tail, lambda i: (l, i) + zeros)
    if out_state is None:
        return in_spec, pl.BlockSpec((DEPTH, nb) + tail, lambda i: (0, i) + zeros), [], [], {}
    out_spec = pl.BlockSpec((1, nb) + tail, lambda i: (l, i) + zeros)
    return in_spec, out_spec, [pl.BlockSpec(memory_space=pl.ANY)], [out_state], None


def _hgrn_step(proj, state, lbs, hg_norm, out_state, l):
    nb = HG_STEP_NB
    blk = lambda c0: pl.BlockSpec((DEC_BATCH, D_MODEL), lambda i: (0, c0 // D_MODEL))
    sspec, ospec, extra_specs, extra_args, aliases = _stacked_state_specs(state, nb, l, out_state)
    in_specs = [
        blk(COL_Q), blk(COL_F), blk(COL_I), blk(COL_OG),
        pl.BlockSpec((1, 1, D_MODEL), lambda i: (l, 0, 0)),
        pl.BlockSpec((1, 1, HG_DV), lambda i: (l, 0, 0)),
        sspec,
    ] + extra_specs
    args = [proj, proj, proj, proj, lbs, hg_norm, state] + extra_args
    if aliases is None:
        aliases = {len(args) - 1: 1}

    def body(*refs):
        ins, rest = refs[:7], refs[len(args):]
        _hgrn_step_kernel(*ins, *rest, first=(l == 0), zero_rest=(out_state is None))

    return pl.pallas_call(
        body,
        grid=(DEC_BATCH // nb,),
        in_specs=in_specs,
        out_specs=[pl.BlockSpec((DEC_BATCH, D_MODEL), lambda i: (0, 0)), ospec],
        out_shape=[
            jax.ShapeDtypeStruct((DEC_BATCH, D_MODEL), F32),
            jax.ShapeDtypeStruct(state.shape, F32),
        ],
        scratch_shapes=[pltpu.VMEM((D_MODEL, DEC_BATCH), F32), pltpu.VMEM((D_MODEL, DEC_BATCH), BF16),
                        pltpu.VMEM((DEC_BATCH, D_MODEL), F32)],
        input_output_aliases=aliases,
        compiler_params=_cparams("arbitrary"),
        name="hgrn_step",
    )(*args)


def _ssm_prep_kernel(xs_ref, bm_ref, cm_ref, dt_ref, cst_ref, w_ref, cb_ref,
                     alog_ref, exp_ref,
                     xc_ref, bc_ref, cc_ref, dec_ref, xdtt_ref, cnew_ref):
    ch = SSM_CONV_CH
    parts = ((xs_ref, 0, SSM_INNER, xc_ref),
             (bm_ref, SSM_INNER, SSM_INNER + 1024, bc_ref),
             (cm_ref, SSM_INNER + 1024, ch, cc_ref))
    for u_ref, c0, c1, out_ref in parts:
        u = u_ref[...]
        acc = cb_ref[:, c0:c1] + w_ref[SSM_CONV - 1:SSM_CONV, c0:c1] * u
        for j in range(SSM_CONV - 1):
            acc = acc + w_ref[j:j + 1, c0:c1] * cst_ref[:, j * ch + c0:j * ch + c1]
        out_ref[...] = _silu(acc)
        cnew_ref[:, (SSM_CONV - 2) * ch + c0:(SSM_CONV - 2) * ch + c1] = u
    cnew_ref[:, 0:(SSM_CONV - 2) * ch] = cst_ref[:, ch:(SSM_CONV - 1) * ch]
    dtv = _dot_sel(dt_ref[...], exp_ref[...])
    dec_ref[...] = jnp.exp(dtv * -jnp.exp(alog_ref[...]))
    xdt = xc_ref[...] * dtv
    for m in range(SSM_INNER // LANE):
        ms = slice(m * LANE, (m + 1) * LANE)
        xdtt_ref[ms, :] = xdt[:, ms].T.astype(BF16)


def _ssm_prep(proj, dt, conv_state, p, l):
    n = DEC_BATCH
    full = lambda shape: pl.BlockSpec(shape, lambda i: tuple(0 for _ in shape))
    lspec = lambda rows, width: pl.BlockSpec((1, rows, width), lambda i: (l, 0, 0))

    def body(xs, bm, cm, dtr, cst, w, cb, alog, ex, *outs):
        _ssm_prep_kernel(xs, bm, cm, dtr, cst.at[0], w.at[0], cb.at[0], alog.at[0], ex, *outs)

    return pl.pallas_call(
        body,
        grid=(1,),
        in_specs=[
            pl.BlockSpec((n, SSM_INNER), lambda i: (0, COL_XBC // SSM_INNER)),
            pl.BlockSpec((n, 1024), lambda i: (0, (COL_XBC + SSM_INNER) // 1024)),
            pl.BlockSpec((n, 1024), lambda i: (0, (COL_XBC + SSM_INNER) // 1024 + 1)),
            full((n, DT_PAD)),
            pl.BlockSpec((1, n, (SSM_CONV - 1) * SSM_CONV_CH), lambda i: (l, 0, 0)),
            lspec(SSM_CONV, SSM_CONV_CH), lspec(1, SSM_CONV_CH),
            lspec(1, SSM_INNER),
            full((DT_PAD, SSM_INNER)),
        ],
        out_specs=[
            full((n, SSM_INNER)), full((n, 1024)), full((n, 1024)),
            full((n, SSM_INNER)), full((SSM_INNER, n)),
            full((n, (SSM_CONV - 1) * SSM_CONV_CH)),
        ],
        out_shape=[
            jax.ShapeDtypeStruct((n, SSM_INNER), F32),
            jax.ShapeDtypeStruct((n, 1024), F32),
            jax.ShapeDtypeStruct((n, 1024), F32),
            jax.ShapeDtypeStruct((n, SSM_INNER), F32),
            jax.ShapeDtypeStruct((SSM_INNER, n), BF16),
            jax.ShapeDtypeStruct((n, (SSM_CONV - 1) * SSM_CONV_CH), F32),
        ],
        compiler_params=_cparams("arbitrary"),
        name="ssm_prep",
    )(proj, proj, proj, dt, conv_state, p['conv_w'], p['conv_b'], p['a_log_x'], p['expand_all'])


SSM_STEP_NB = 4


def _ssm_step_kernel(dec_ref, xdtt_ref, bc_ref, cc_ref, s_ref, y_ref, so_ref, *, zero_rest):
    i = pl.program_id(0)

    @pl.when(i == 0)
    def _():
        y_ref[...] = jnp.zeros_like(y_ref)

    if zero_rest:
        so_ref[1:] = jnp.zeros((DEPTH - 1,) + so_ref.shape[1:], F32)

    def body(j, carry):
        bidx = i * SSM_STEP_NB + j
        for g in range(SSM_GROUPS):
            rs = slice(g * GROUP_W, (g + 1) * GROUP_W)
            hs = slice(g * SSM_HPG, (g + 1) * SSM_HPG)
            ns = slice(g * SSM_STATE, (g + 1) * SSM_STATE)
            dec_row = _get_row(dec_ref, bidx, rs)
            decay = jnp.concatenate(
                [jnp.broadcast_to(dec_row[:, k * SSM_HEADDIM:k * SSM_HEADDIM + 1],
                                  (SSM_HEADDIM, SSM_STATE)) for k in range(SSM_HPG)], axis=0)
            inc = _dot(xdtt_ref[rs, :], _one_hot_rows(_get_row(bc_ref, bidx, ns), bidx))
            h = s_ref[0, j, hs].reshape(GROUP_W, SSM_STATE)
            h_new = decay * h + inc
            so_ref[0, j, hs] = h_new.reshape(SSM_HPG, SSM_HEADDIM, SSM_STATE)
            y_row = _row_times(_get_row(cc_ref, bidx, ns), mat_nt=h_new.astype(BF16))
            _set_row(y_ref, bidx, rs, y_row)
        return carry

    lax.fori_loop(0, SSM_STEP_NB, body, 0)


def _ssm_step(dec, xdtt, bc, cc, state, out_state, l):
    nb = SSM_STEP_NB
    full = lambda shape: pl.BlockSpec(shape, lambda i: tuple(0 for _ in shape))
    sspec, ospec, extra_specs, extra_args, aliases = _stacked_state_specs(state, nb, l, out_state)
    in_specs = [full(dec.shape), full(xdtt.shape), full(bc.shape), full(cc.shape),
                sspec] + extra_specs
    args = [dec, xdtt, bc, cc, state] + extra_args
    if aliases is None:
        aliases = {len(args) - 1: 1}

    def body(*refs):
        ins, rest = refs[:5], refs[len(args):]
        _ssm_step_kernel(*ins, *rest, zero_rest=(out_state is None))

    return pl.pallas_call(
        body,
        grid=(DEC_BATCH // nb,),
        in_specs=in_specs,
        out_specs=[full((DEC_BATCH, SSM_INNER)), ospec],
        out_shape=[
            jax.ShapeDtypeStruct((DEC_BATCH, SSM_INNER), F32),
            jax.ShapeDtypeStruct(state.shape, F32),
        ],
        input_output_aliases=aliases,
        compiler_params=_cparams("arbitrary"),
        name="ssm_step",
    )(*args)


def _ssm_post_kernel(y_in_ref, xc_ref, z_ref, dsk_ref, nw_ref, y_ref):
    for g in range(SSM_GROUPS):
        gs = slice(g * GROUP_W, (g + 1) * GROUP_W)
        y = (y_in_ref[:, gs] + dsk_ref[:, gs] * xc_ref[:, gs]) * _silu(z_ref[:, gs])
        y_ref[:, gs] = _rms(y, nw_ref[:, gs])


def _ssm_post(y, xc, proj, p, l):
    n = DEC_BATCH
    full = lambda shape: pl.BlockSpec(shape, lambda i: tuple(0 for _ in shape))
    lspec = pl.BlockSpec((1, 1, SSM_INNER), lambda i: (l, 0, 0))

    def body(y_in, xc_r, z_r, dsk, nw, y_r):
        _ssm_post_kernel(y_in, xc_r, z_r, dsk.at[0], nw.at[0], y_r)

    return pl.pallas_call(
        body,
        grid=(1,),
        in_specs=[full((n, SSM_INNER)), full((n, SSM_INNER)),
                  pl.BlockSpec((n, SSM_INNER), lambda i: (0, COL_Z // SSM_INNER)),
                  lspec, lspec],
        out_specs=full((n, SSM_INNER)),
        out_shape=jax.ShapeDtypeStruct((n, SSM_INNER), F32),
        compiler_params=_cparams("arbitrary"),
        name="ssm_post",
    )(y, xc, proj, p['d_skip_x'], p['ssm_norm'])


def _merge_kernel(x_ref, oh_ref, y_ref, ga_ref, gb_ref, g1_ref, bma_ref, bmb_ref,
                  wa_ref, wb_ref, wo_ref, o_ref):
    a = _dot(oh_ref[...].astype(BF16), wa_ref[0])
    b = _dot(y_ref[...].astype(BF16), wb_ref[0])
    u = _sigmoid(ga_ref[...] + bma_ref[0]) * a + _sigmoid(gb_ref[...] + bmb_ref[0]) * b
    o_ref[...] = x_ref[...] + g1_ref[0] * _dot(u.astype(BF16), wo_ref[0])


def _merge(x, o_hg, y, proj, mod, p, l, tm, rows_per_group):
    rows = x.shape[0]
    wspec = lambda k: pl.BlockSpec((1, k, D_MODEL), lambda i: (l, 0, 0))
    return pl.pallas_call(
        _merge_kernel,
        grid=(rows // tm,),
        in_specs=[
            pl.BlockSpec((tm, D_MODEL), lambda i: (i, 0)),
            pl.BlockSpec((tm, D_MODEL), lambda i: (i, 0)),
            pl.BlockSpec((tm, SSM_INNER), lambda i: (i, 0)),
            pl.BlockSpec((tm, D_MODEL), lambda i: (i, COL_GA // D_MODEL)),
            pl.BlockSpec((tm, D_MODEL), lambda i: (i, COL_GB // D_MODEL)),
            _mod_spec(mod, tm, rows_per_group, ADA_G1),
            pl.BlockSpec((1, 1, D_MODEL), lambda i: (l, 0, 0)),
            pl.BlockSpec((1, 1, D_MODEL), lambda i: (l, 0, 1)),
            wspec(D_MODEL), wspec(SSM_INNER), wspec(D_MODEL),
        ],
        out_specs=pl.BlockSpec((tm, D_MODEL), lambda i: (i, 0)),
        out_shape=jax.ShapeDtypeStruct((rows, D_MODEL), F32),
        compiler_params=_cparams("arbitrary"),
        name="merge",
    )(x, o_hg, y, proj, proj, mod, p['b_merge'], p['b_merge'], p['w_br_a'], p['w_br_b'],
      p['w_out'])


FF_CHUNK = 1024


def _mlp_kernel(x_ref, sc_ref, sh_ref, g_ref, nw_ref, wu_ref, wd_ref, fw_ref, o_ref, *, final):
    x = x_ref[...]
    h = (_rms(x, nw_ref[0]) * (1.0 + sc_ref[0]) + sh_ref[0]).astype(BF16)
    acc = jnp.zeros(x.shape, F32)
    for c in range(D_FF // FF_CHUNK):
        cs = slice(c * FF_CHUNK, (c + 1) * FF_CHUNK)
        a = jnp.maximum(_dot(h, wu_ref[0, :, cs]), 0.0)
        acc = acc + _dot((a * a).astype(BF16), wd_ref[0, cs, :])
    out = x + g_ref[0] * acc
    if final:
        out = _rms(out, fw_ref[...])
    o_ref[...] = out


def _mlp(x, mod, p, norm_final, l, tm, rows_per_group):
    rows = x.shape[0]
    return pl.pallas_call(
        functools.partial(_mlp_kernel, final=(l == DEPTH - 1)),
        grid=(rows // tm,),
        in_specs=[
            pl.BlockSpec((tm, D_MODEL), lambda i: (i, 0)),
            _mod_spec(mod, tm, rows_per_group, ADA_SC2),
            _mod_spec(mod, tm, rows_per_group, ADA_SH2),
            _mod_spec(mod, tm, rows_per_group, ADA_G2),
            pl.BlockSpec((1, 1, D_MODEL), lambda i: (l, 0, 0)),
            pl.BlockSpec((1, D_MODEL, D_FF), lambda i: (l, 0, 0)),
            pl.BlockSpec((1, D_FF, D_MODEL), lambda i: (l, 0, 0)),
            pl.BlockSpec((1, D_MODEL), lambda i: (0, 0)),
        ],
        out_specs=pl.BlockSpec((tm, D_MODEL), lambda i: (i, 0)),
        out_shape=jax.ShapeDtypeStruct((rows, D_MODEL), F32),
        compiler_params=_cparams("arbitrary"),
        name="mlp",
    )(x, mod, mod, mod, p['norm_mlp'], p['w_up'], p['w_down'], norm_final)


def _prepare_params(w_ada, b_ada, norm_mix, w_in, b_merge, hg_norm, conv_w, conv_b, dt_bias,
                    a_log, d_skip, ssm_norm, w_br_a, w_br_b, w_out, norm_mlp, w_up, w_down):
    dt0 = COL_XBC + SSM_CONV_CH
    w_main = jnp.concatenate([w_in[:, :, :dt0], w_in[:, :, dt0 + SSM_HEADS:]], axis=-1)
    w_dt = jnp.pad(w_in[:, :, dt0:dt0 + SSM_HEADS], ((0, 0), (0, 0), (0, DT_PAD - SSM_HEADS)))
    per_head = lambda a: jnp.repeat(a, SSM_HEADDIM, axis=-1).reshape(DEPTH, 1, SSM_INNER)
    narrow = lambda a: jnp.pad(a, ((0, 0), (0, DT_PAD - SSM_HEADS))).reshape(DEPTH, 1, DT_PAD)
    head_of_col = jnp.arange(SSM_INNER) // SSM_HEADDIM
    expand_all = (jnp.arange(DT_PAD)[:, None] == head_of_col[None, :]).astype(BF16)
    expand_g = expand_all.reshape(DT_PAD, SSM_GROUPS, GROUP_W).transpose(1, 0, 2)
    return {
        'norm_mix': norm_mix.reshape(DEPTH, 1, D_MODEL),
        'w_main': w_main.astype(BF16), 'w_dt': w_dt.astype(BF16),
        'b_merge': b_merge.reshape(DEPTH, 1, 2 * D_MODEL),
        'hg_norm': hg_norm.reshape(DEPTH, 1, HG_DV),
        'conv_w': conv_w, 'conv_b': conv_b.reshape(DEPTH, 1, SSM_CONV_CH),
        'dt_bias_n': narrow(dt_bias), 'a_log_n': narrow(a_log),
        'a_log_x': per_head(a_log), 'd_skip_x': per_head(d_skip),
        'ssm_norm': ssm_norm.reshape(DEPTH, 1, SSM_INNER),
        'expand_all': expand_all, 'expand_g': expand_g,
        'w_br_a': w_br_a.astype(BF16), 'w_br_b': w_br_b.astype(BF16),
        'w_out': w_out.astype(BF16),
        'norm_mlp': norm_mlp.reshape(DEPTH, 1, D_MODEL),
        'w_up': w_up.astype(BF16), 'w_down': w_down.astype(BF16),
    }


def kernel(x_prompt, x_sample, state_hgrn, state_ssm, state_conv, c_prompt, c_sample, w_ada, b_ada, norm_mix, w_in, b_merge, lower_bounds, hg_norm, conv_w, conv_b, dt_bias, a_log, d_skip, ssm_norm, w_br_a, w_br_b, w_out, norm_mlp, w_up, w_down, norm_final):
    p = _prepare_params(w_ada, b_ada, norm_mix, w_in, b_merge, hg_norm, conv_w, conv_b, dt_bias,
                        a_log, d_skip, ssm_norm, w_br_a, w_br_b, w_out, norm_mlp, w_up, w_down)
    lbs = _lower_bounds(lower_bounds).reshape(DEPTH, 1, HG_HEADS * HG_DK)
    mod = _modulation(jnp.concatenate([c_prompt, c_sample], axis=0), w_ada, b_ada)
    mod_p = mod[:, :BATCH].reshape(DEPTH, BATCH, 1, N_ADA * D_MODEL)
    mod_s = mod[:, BATCH:].reshape(DEPTH, 1, DEC_BATCH, N_ADA * D_MODEL)
    norm_final2 = norm_final.reshape(1, D_MODEL)
    conv_state = state_conv.reshape(DEPTH, DEC_BATCH, (SSM_CONV - 1) * SSM_CONV_CH)

    xp = x_prompt.reshape(BATCH * SEQ, D_MODEL)
    xs = x_sample.reshape(DEC_BATCH, D_MODEL)
    hg_p, ssm_p, cv_p, cv_s = [], [], [], []
    hg_s = ssm_s = None
    for l in range(DEPTH):
        proj, dt, cum = _inproj(xp, mod_p[l], p, l, SEQ, SEQ)
        o_hg, s_new, y, h_new = _mixer_prompt(proj, dt, cum, lbs, p, l)
        hg_p.append(s_new)
        ssm_p.append(h_new)
        cv_p.append(proj.reshape(BATCH, SEQ, PROJ_W)[:, SEQ - (SSM_CONV - 1):,
                                                       COL_XBC:COL_XBC + SSM_CONV_CH])
        xp = _merge(xp, o_hg, y, proj, mod_p[l], p, l, 512, SEQ)
        xp = _mlp(xp, mod_p[l], p, norm_final2, l, 512, SEQ)
        proj, dt, _ = _inproj(xs, mod_s[l], p, l, DEC_BATCH, DEC_BATCH)
        o_hg, hg_s = _hgrn_step(proj, state_hgrn, lbs, p['hg_norm'], hg_s, l)
        xc, bc, cc, dect, xdtt, cnew = _ssm_prep(proj, dt, conv_state, p, l)
        yt, ssm_s = _ssm_step(dect, xdtt, bc, cc, state_ssm, ssm_s, l)
        y = _ssm_post(yt, xc, proj, p, l)
        cv_s.append(cnew.reshape(DEC_BATCH, SSM_CONV - 1, SSM_CONV_CH))
        xs = _merge(xs, o_hg, y, proj, mod_s[l], p, l, DEC_BATCH, DEC_BATCH)
        xs = _mlp(xs, mod_s[l], p, norm_final2, l, DEC_BATCH, DEC_BATCH)
    return (xp.reshape(BATCH, SEQ, D_MODEL), xs.reshape(DEC_BATCH, 1, D_MODEL),
            jnp.stack(hg_p), jnp.stack(ssm_p), jnp.stack(cv_p),
            hg_s, ssm_s, jnp.stack(cv_s))
```

```python
import functools

import jax
import jax.numpy as jnp
import numpy as np
from jax import lax
from jax.experimental import pallas as pl
from jax.experimental.pallas import tpu as pltpu

F32 = jnp.float32
BF16 = jnp.bfloat16

D_MODEL = 1024
BATCH = 8
SEQ = 2048
DEPTH = 4
DEC_BATCH = 128
HG_HEADS = 8
HG_DK = 128
HG_DV = 128
SSM_INNER = 2048
SSM_HEADDIM = 64
SSM_HEADS = 32
SSM_GROUPS = 8
SSM_HPG = 4
SSM_STATE = 128
SSM_CONV = 4
SSM_CONV_CH = 4096
GROUP_W = SSM_HPG * SSM_HEADDIM
D_FF = 4096
N_ADA = 6
EPS = 1e-6
DT_PAD = 128
COL_Q, COL_F, COL_I, COL_OG = 0, 1024, 2048, 3072
COL_Z, COL_XBC, COL_GA, COL_GB = 4096, 6144, 10240, 11264
PROJ_W = 12288
ADA_SH1, ADA_SC1, ADA_G1, ADA_SH2, ADA_SC2, ADA_G2 = range(6)

LANE = 128
SUBLANES = 8
TILE = 128
VMEM_LIMIT = 56 * 1024 * 1024


def _cparams(*sem):
    return pltpu.CompilerParams(dimension_semantics=sem, vmem_limit_bytes=VMEM_LIMIT)


def _sigmoid(x):
    return 0.5 * jnp.tanh(0.5 * x) + 0.5


def _silu(x):
    return x * _sigmoid(x)


def _softplus(x):
    return jnp.maximum(x, 0.0) + jnp.log1p(jnp.exp(-jnp.abs(x)))


def _dot(a, b):
    return jnp.dot(a, b, preferred_element_type=F32)


def _dot_nt(a, b):
    return lax.dot_general(a, b, (((1,), (1,)), ((), ())), preferred_element_type=F32)


def _split(a, terms):
    parts = []
    for _ in range(terms - 1):
        hi = a.astype(BF16)
        parts.append(hi)
        a = a - hi.astype(F32)
    parts.append(a.astype(BF16))
    return parts


def _sel_dot(sel, a, terms=3):
    return _dot(jnp.concatenate([sel] * terms, axis=1), jnp.concatenate(_split(a, terms), axis=0))


def _dot_sel(a, sel, terms=3):
    return _dot(jnp.concatenate(_split(a, terms), axis=1), jnp.concatenate([sel] * terms, axis=0))


def _rms(x, w):
    ms = jnp.mean(x * x, axis=-1, keepdims=True)
    return x * lax.rsqrt(ms + EPS) * w


def _hgrn_gates(f, lb, first):
    if first:
        g = jnp.minimum(f, 0.0) - jnp.log1p(jnp.exp(-jnp.abs(f)))
        k = _sigmoid(-f)
    else:
        fg = lb + (1.0 - lb) * _sigmoid(f)
        g = jnp.log(fg)
        k = 1.0 - fg
    return g, k


def _lbs_kernel(lb_ref, o_ref):
    x = lb_ref[...]
    m = jnp.max(x, axis=0, keepdims=True)
    e = jnp.exp(x - m)
    p = e / jnp.sum(e, axis=0, keepdims=True)
    acc = jnp.zeros_like(p[0:1])
    rows = [acc]
    for l in range(1, DEPTH):
        acc = acc + p[l:l + 1]
        rows.append(acc)
    o_ref[...] = jnp.concatenate(rows, axis=0)


def _lower_bounds(lower_bounds):
    return pl.pallas_call(
        _lbs_kernel, out_shape=jax.ShapeDtypeStruct(lower_bounds.shape, F32), name="lbs",
    )(lower_bounds)


MOD_STREAMS = 3


def _mod_kernel(c_ref, *refs):
    w_refs, b_ref, o_ref = refs[:MOD_STREAMS], refs[MOD_STREAMS], refs[MOD_STREAMS + 1]
    a = _silu(c_ref[...]).astype(BF16)
    for k, w_ref in enumerate(w_refs):
        cs = slice(k * D_MODEL, (k + 1) * D_MODEL)
        o_ref[0, :, cs] = _dot(a, w_ref[0].astype(BF16)) + b_ref[0, :, cs]


def _modulation(c_all, w_ada, b_ada):
    n = c_all.shape[0]
    tn = MOD_STREAMS * D_MODEL
    w_spec = lambda k: pl.BlockSpec((1, D_MODEL, D_MODEL), lambda l, j: (l, 0, j * MOD_STREAMS + k))
    return pl.pallas_call(
        _mod_kernel,
        grid=(DEPTH, N_ADA * D_MODEL // tn),
        in_specs=[pl.BlockSpec((n, D_MODEL), lambda l, j: (0, 0))]
        + [w_spec(k) for k in range(MOD_STREAMS)]
        + [pl.BlockSpec((1, 1, tn), lambda l, j: (l, 0, j))],
        out_specs=pl.BlockSpec((1, n, tn), lambda l, j: (l, 0, j)),
        out_shape=jax.ShapeDtypeStruct((DEPTH, n, N_ADA * D_MODEL), F32),
        compiler_params=_cparams("arbitrary", "arbitrary"),
        name="modulation",
    )(c_all, *([w_ada] * MOD_STREAMS), b_ada.reshape(DEPTH, 1, N_ADA * D_MODEL))


def _mod_spec(mod, tm, rows_per_group, piece):
    if mod.shape[1] == 1:
        tiles = rows_per_group // tm
        return pl.BlockSpec((1, 1, D_MODEL), lambda i, *_: (i // tiles, 0, piece))
    return pl.BlockSpec((1, tm, D_MODEL), lambda i, *_: (0, i, piece))


def _inproj_kernel(x_ref, sc_ref, sh_ref, nw_ref, w_ref, wg_ref, wdt_ref, dtb_ref, alog_ref,
                   o_ref, odt_ref, ocum_ref, h_scr, *, left_blocks):
    j = pl.program_id(1)

    @pl.when(j == 0)
    def _():
        h = _rms(x_ref[...], nw_ref[0]) * (1.0 + sc_ref[0]) + sh_ref[0]
        hb = h.astype(BF16)
        h_scr[...] = hb
        dtv = _softplus(_dot(hb, wdt_ref[0]) + dtb_ref[0])
        odt_ref[...] = dtv
        la = dtv * -jnp.exp(alog_ref[0])
        row = lax.broadcasted_iota(jnp.int32, (TILE, TILE), 0)
        col = lax.broadcasted_iota(jnp.int32, (TILE, TILE), 1)
        tri = jnp.where(row >= col, 1.0, 0.0).astype(BF16)
        for i in range(la.shape[0] // TILE):
            rs = slice(i * TILE, (i + 1) * TILE)
            ocum_ref[rs, :] = _sel_dot(tri, la[rs] * LOG2E)

    @pl.when(j < left_blocks)
    def _():
        o_ref[...] = _dot(h_scr[...], w_ref[0])

    @pl.when(j >= left_blocks)
    def _():
        o_ref[...] = _dot(h_scr[...], wg_ref[0])


def _inproj(x, mod, p, l, tm, rows_per_group):
    rows = x.shape[0]
    tn = 1024
    left_blocks = COL_GA // tn
    narrow = pl.BlockSpec((1, 1, DT_PAD), lambda i, j: (l, 0, 0))
    return pl.pallas_call(
        functools.partial(_inproj_kernel, left_blocks=left_blocks),
        grid=(rows // tm, PROJ_W // tn),
        in_specs=[
            pl.BlockSpec((tm, D_MODEL), lambda i, j: (i, 0)),
            _mod_spec(mod, tm, rows_per_group, ADA_SC1),
            _mod_spec(mod, tm, rows_per_group, ADA_SH1),
            pl.BlockSpec((1, 1, D_MODEL), lambda i, j: (l, 0, 0)),
            pl.BlockSpec((1, D_MODEL, tn), lambda i, j: (l, 0, jnp.minimum(j, left_blocks - 1))),
            pl.BlockSpec((1, D_MODEL, tn), lambda i, j: (l, 0, jnp.maximum(j - left_blocks, 0))),
            pl.BlockSpec((1, D_MODEL, DT_PAD), lambda i, j: (l, 0, 0)),
            narrow, narrow,
        ],
        out_specs=[
            pl.BlockSpec((tm, tn), lambda i, j: (i, j)),
            pl.BlockSpec((tm, DT_PAD), lambda i, j: (i, 0)),
            pl.BlockSpec((tm, DT_PAD), lambda i, j: (i, 0)),
        ],
        out_shape=[
            jax.ShapeDtypeStruct((rows, PROJ_W), F32),
            jax.ShapeDtypeStruct((rows, DT_PAD), F32),
            jax.ShapeDtypeStruct((rows, DT_PAD), F32),
        ],
        scratch_shapes=[pltpu.VMEM((tm, D_MODEL), BF16)],
        compiler_params=_cparams("arbitrary", "arbitrary"),
        name="inproj",
    )(x, mod, mod, p['norm_mix'], p['w_in'], p['w_gate'], p['w_dt'], p['dt_bias_n'], p['a_log_n'])


HG_LEVELS = 7
HG_HB = 4
HG_SPLIT = 2
LOG2E = 1.4426950408889634


def _hgrn_sum_ranges():
    r = np.arange(TILE)[:, None]
    c = np.arange(TILE)[None, :]
    mats = [c <= r, c > r]
    for s in range(HG_LEVELS):
        h = 1 << s
        right = ((r >> s) & 1) == 1
        in_right = (c >= (r & ~(h - 1))) & (c <= r)
        in_left = (c > r) & (c <= (r | (h - 1)))
        mats.append(np.where(right, in_right, in_left))
    m = np.concatenate(mats, axis=0).astype(np.float32)
    return np.tile(m, (1, HG_SPLIT))


def _hgrn_level_masks():
    r = np.arange(TILE)[:, None]
    c = np.arange(TILE)[None, :]
    x = r ^ c
    top = np.where(x > 0, np.floor(np.log2(np.maximum(x, 1))).astype(np.int64), -1)
    masks = [r == c] + [(r > c) & (top == s) for s in range(HG_LEVELS)]
    return np.stack(masks).astype(np.float32)


def _hgrn_tile(rows, q_ref, f_ref, v_ref, og_ref, lb, nw, sums_ref, mask_ref, o_ref, st_scr, first):
    g, k = _hgrn_gates(f_ref[rows, :], lb, first)
    g2 = jnp.concatenate(_split(g * LOG2E, HG_SPLIT), axis=0)
    decay = jnp.exp2(_dot(sums_ref[...], g2))
    q = _silu(q_ref[rows, :]) * (HG_DK ** -0.5)
    v = v_ref[rows, :]
    og = _silu(og_ref[rows, :])
    q_in = (q * decay[0:TILE]).astype(BF16)
    k_out = (k * decay[TILE:2 * TILE]).astype(BF16)
    end = decay[TILE - 1:TILE]
    qb = q.astype(BF16)
    kb = k.astype(BF16)
    vb = v.astype(BF16)
    outs = []
    for hh in range(HG_HB):
        cs = slice(hh * HG_DK, (hh + 1) * HG_DK)
        st = st_scr[hh]
        p = mask_ref[0] * _dot_nt(qb[:, cs], kb[:, cs])
        for s in range(HG_LEVELS):
            a = decay[(2 + s) * TILE:(3 + s) * TILE, cs].astype(BF16)
            p = p + mask_ref[1 + s] * _dot_nt(qb[:, cs] * a, kb[:, cs] * a)
        o = _dot_nt(q_in[:, cs], st.astype(BF16)) + _dot(p.astype(BF16), vb[:, cs])
        st_scr[hh] = st * end[:, cs] + _dot(v[:, cs].T.astype(BF16), k_out[:, cs])
        outs.append(_rms(o, nw) * og[:, cs])
    o_ref[rows, :] = jnp.concatenate(outs, axis=1).astype(o_ref.dtype)


def _conv_silu_tile(ref, cols, w_ref, b_ref, tail, r0, t):
    cur = ref[pl.ds(r0, TILE), cols]
    prev = ref[pl.ds(pl.multiple_of(jnp.maximum(r0 - SUBLANES, 0), SUBLANES), SUBLANES), cols]
    prev = jnp.where(t > 0, prev, tail)
    ext = jnp.concatenate([prev, cur], axis=0)
    acc = b_ref[:, cols] + w_ref[SSM_CONV - 1:SSM_CONV, cols] * cur
    for j in range(SSM_CONV - 1):
        acc = acc + w_ref[j:j + 1, cols] * ext[5 + j:5 + j + TILE, :]
    return _silu(acc)


def _ssd_tile(rows, r0, t, gi, x_ref, bm_ref, cm_ref, z_ref, dt_ref, cum_ref,
              wx_ref, wb_ref, wc_ref, cbx_ref, cbb_ref, cbc_ref, dsk_ref, nw_ref, exp_ref,
              tail_scr, y_ref, ht_scr, causal, head_of_lane):
    xs = slice(gi * GROUP_W, (gi + 1) * GROUP_W)
    ns = slice(gi * SSM_STATE, (gi + 1) * SSM_STATE)
    b_off = MIX_GROUPS * GROUP_W
    c_off = b_off + MIX_GROUPS * SSM_STATE
    tail_b = slice(b_off + gi * SSM_STATE, b_off + (gi + 1) * SSM_STATE)
    tail_c = slice(c_off + gi * SSM_STATE, c_off + (gi + 1) * SSM_STATE)
    xc = _conv_silu_tile(x_ref, xs, wx_ref, cbx_ref, tail_scr[:, xs], r0, t)
    bc = _conv_silu_tile(bm_ref, ns, wb_ref, cbb_ref, tail_scr[:, tail_b], r0, t)
    cc = _conv_silu_tile(cm_ref, ns, wc_ref, cbc_ref, tail_scr[:, tail_c], r0, t)
    expand = exp_ref[gi]
    dtv = _dot_sel(dt_ref[rows, :], expand, terms=2)
    cum = _dot_sel(cum_ref[rows, :], expand)
    cum_end = cum[TILE - 1:TILE, :]
    cum_t = cum.T
    xdt = xc * dtv
    xdtb = xdt.astype(BF16)
    ccb = cc.astype(BF16)
    cb = jnp.where(causal, _dot_nt(ccb, bc.astype(BF16)), 0.0)
    ht = ht_scr[gi]
    weights, values = [], []
    for k in range(SSM_HPG):
        c_col = cum[:, k * SSM_HEADDIM:k * SSM_HEADDIM + 1]
        c_row = cum_t[k * SSM_HEADDIM:k * SSM_HEADDIM + 1, :]
        decay = jnp.exp2(jnp.minimum(c_col - c_row, 0.0))
        weights.append((cb * decay).astype(BF16))
        values.append(jnp.where(head_of_lane == k, xdtb, jnp.zeros_like(xdtb)))
    y = (_dot(ccb, ht.astype(BF16)) * jnp.exp2(cum)
         + _dot(jnp.concatenate(weights, axis=1), jnp.concatenate(values, axis=0)))
    x_out = (xdt * jnp.exp2(cum_end - cum)).astype(BF16)
    ht_scr[gi] = ht * jnp.exp2(cum_end) + _dot(bc.T.astype(BF16), x_out)
    y = (y + dsk_ref[:, xs] * xc) * _silu(z_ref[rows, xs])
    y_ref[rows, xs] = _rms(y, nw_ref[:, xs]).astype(y_ref.dtype)


MIX_ROWS = 512
MIX_GROUPS = SSM_GROUPS // (HG_HEADS // HG_HB)


def _mixer_prompt_kernel(q_ref, f_ref, v_ref, og_ref, lb_ref, hnw_ref, sums_ref, mask_ref,
                         x_ref, bm_ref, cm_ref, z_ref, dt_ref, cum_ref,
                         wx_ref, wb_ref, wc_ref, cbx_ref, cbb_ref, cbc_ref,
                         dsk_ref, snw_ref, exp_ref,
                         o_ref, s_ref, y_ref, h_ref, st_scr, ht_scr, tail_scr, *, first):
    chunk = pl.program_id(2)

    @pl.when(chunk == 0)
    def _():
        st_scr[...] = jnp.zeros_like(st_scr)
        ht_scr[...] = jnp.zeros_like(ht_scr)
        tail_scr[...] = jnp.zeros_like(tail_scr)

    row = lax.broadcasted_iota(jnp.int32, (TILE, TILE), 0)
    col = lax.broadcasted_iota(jnp.int32, (TILE, TILE), 1)
    causal = row >= col
    head_of_lane = lax.broadcasted_iota(jnp.int32, (TILE, GROUP_W), 1) // SSM_HEADDIM
    lb = lb_ref[0]
    hnw = hnw_ref[0]

    def tile_body(t, carry):
        r0 = pl.multiple_of(t * TILE, TILE)
        rows = pl.ds(r0, TILE)
        _hgrn_tile(rows, q_ref, f_ref, v_ref, og_ref, lb, hnw, sums_ref, mask_ref, o_ref, st_scr,
                   first)
        for gi in range(MIX_GROUPS):
            _ssd_tile(rows, r0, t, gi, x_ref, bm_ref, cm_ref, z_ref, dt_ref, cum_ref,
                      wx_ref, wb_ref, wc_ref, cbx_ref, cbb_ref, cbc_ref, dsk_ref, snw_ref, exp_ref,
                      tail_scr, y_ref, ht_scr, causal, head_of_lane)
        return carry

    lax.fori_loop(0, MIX_ROWS // TILE, tile_body, 0, unroll=2)

    last = slice(MIX_ROWS - SUBLANES, MIX_ROWS)
    b_off = MIX_GROUPS * GROUP_W
    c_off = b_off + MIX_GROUPS * SSM_STATE
    tail_scr[:, 0:b_off] = x_ref[last, :]
    tail_scr[:, b_off:c_off] = bm_ref[last, :]
    tail_scr[:, c_off:] = cm_ref[last, :]

    @pl.when(chunk == pl.num_programs(2) - 1)
    def _():
        for hh in range(HG_HB):
            s_ref[0, hh] = st_scr[hh].T
        for gi in range(MIX_GROUPS):
            h_ref[0, gi * SSM_HPG:(gi + 1) * SSM_HPG] = ht_scr[gi].T.reshape(
                SSM_HPG, SSM_HEADDIM, SSM_STATE)


def _mixer_prompt(proj, dt, cum, lbs, p, l):
    rows = MIX_ROWS
    n_chunks = SEQ // rows
    hw = HG_HB * HG_DK
    xw = MIX_GROUPS * GROUP_W
    nw = MIX_GROUPS * SSM_STATE
    b0 = COL_XBC + SSM_INNER
    c0 = b0 + SSM_GROUPS * SSM_STATE
    act = lambda width, col0: pl.BlockSpec(
        (rows, width), lambda b, h, c: (b * n_chunks + c, col0 // width + h))
    par = lambda nrows, width, col0: pl.BlockSpec(
        (1, nrows, width), lambda b, h, c: (l, 0, col0 // width + h))
    narrow = pl.BlockSpec((rows, DT_PAD), lambda b, h, c: (b * n_chunks + c, 0))
    sums = jnp.asarray(_hgrn_sum_ranges(), BF16)
    masks = jnp.asarray(_hgrn_level_masks(), F32)
    in_specs = [
        act(hw, COL_Q), act(hw, COL_F), act(hw, COL_I), act(hw, COL_OG),
        par(1, hw, 0), pl.BlockSpec((1, 1, HG_DV), lambda b, h, c: (l, 0, 0)),
        pl.BlockSpec(sums.shape, lambda b, h, c: (0, 0)),
        pl.BlockSpec(masks.shape, lambda b, h, c: (0, 0, 0)),
        act(xw, COL_XBC), act(nw, b0), act(nw, c0), act(xw, COL_Z), narrow, narrow,
        par(SSM_CONV, xw, 0), par(SSM_CONV, nw, SSM_INNER),
        par(SSM_CONV, nw, SSM_INNER + SSM_GROUPS * SSM_STATE),
        par(1, xw, 0), par(1, nw, SSM_INNER), par(1, nw, SSM_INNER + SSM_GROUPS * SSM_STATE),
        par(1, xw, 0), par(1, xw, 0),
        pl.BlockSpec((MIX_GROUPS, DT_PAD, GROUP_W), lambda b, h, c: (h, 0, 0)),
    ]
    args = (proj, proj, proj, proj, lbs, p['hg_norm'], sums, masks,
            proj, proj, proj, proj, dt, cum,
            p['conv_w'], p['conv_w'], p['conv_w'], p['conv_b'], p['conv_b'], p['conv_b'],
            p['d_skip_x'], p['ssm_norm'], p['expand_g'])

    def body(*refs):
        ins = list(refs[:len(args)])
        for i in range(14, 22):
            ins[i] = ins[i].at[0]
        _mixer_prompt_kernel(*ins, *refs[len(args):], first=(l == 0))

    return pl.pallas_call(
        body,
        grid=(BATCH, HG_HEADS // HG_HB, n_chunks),
        in_specs=in_specs,
        out_specs=[
            pl.BlockSpec((rows, hw), lambda b, h, c: (b * n_chunks + c, h)),
            pl.BlockSpec((1, HG_HB, HG_DK, HG_DV), lambda b, h, c: (b, h, 0, 0)),
            pl.BlockSpec((rows, xw), lambda b, h, c: (b * n_chunks + c, h)),
            pl.BlockSpec((1, MIX_GROUPS * SSM_HPG, SSM_HEADDIM, SSM_STATE),
                         lambda b, h, c: (b, h, 0, 0)),
        ],
        out_shape=[
            jax.ShapeDtypeStruct((BATCH * SEQ, HG_HEADS * HG_DV), BF16),
            jax.ShapeDtypeStruct((BATCH, HG_HEADS, HG_DK, HG_DV), F32),
            jax.ShapeDtypeStruct((BATCH * SEQ, SSM_INNER), BF16),
            jax.ShapeDtypeStruct((BATCH, SSM_HEADS, SSM_HEADDIM, SSM_STATE), F32),
        ],
        scratch_shapes=[
            pltpu.VMEM((HG_HB, HG_DV, HG_DK), F32),
            pltpu.VMEM((MIX_GROUPS, SSM_STATE, GROUP_W), F32),
            pltpu.VMEM((SUBLANES, xw + 2 * nw), F32),
        ],
        compiler_params=_cparams("arbitrary", "arbitrary", "arbitrary"),
        name="mixer_prompt",
    )(*args)


HG_STEP_NB = 8


def _column(mat_t, lane_mask):
    return jnp.sum(jnp.where(lane_mask, mat_t, 0.0), axis=1, keepdims=True)


def _row_group(ref, bidx, cols):
    start = pl.multiple_of((bidx // SUBLANES) * SUBLANES, SUBLANES)
    idx = (pl.ds(start, SUBLANES), cols)
    blk = ref[idx]
    is_row = lax.broadcasted_iota(jnp.int32, blk.shape, 0) == bidx % SUBLANES
    return idx, blk, is_row


def _get_row(ref, bidx, cols):
    _, blk, is_row = _row_group(ref, bidx, cols)
    return jnp.sum(jnp.where(is_row, blk, 0.0), axis=0, keepdims=True)


def _set_row(ref, bidx, cols, row):
    idx, blk, is_row = _row_group(ref, bidx, cols)
    ref[idx] = jnp.where(is_row, row, blk)


BF16_ROWS = 16


def _one_hot_rows(row, bidx):
    sub = lax.broadcasted_iota(jnp.int32, (DEC_BATCH, row.shape[1]), 0)
    return jnp.where(sub == bidx, row, 0.0).astype(BF16)


def _row_times(row, mat_nt=None, mat=None):
    lhs = jnp.broadcast_to(row.astype(BF16), (BF16_ROWS, row.shape[1]))
    out = _dot(lhs, mat) if mat is not None else _dot_nt(lhs, mat_nt)
    return out[0:1]


def _hgrn_step_kernel(q_ref, f_ref, v_ref, og_ref, lb_ref, nw_ref, s_ref, o_ref, so_ref,
                      dec_t, k_t, q_s, *, first, zero_rest):
    i = pl.program_id(0)

    @pl.when(i == 0)
    def _():
        g, k = _hgrn_gates(f_ref[...], lb_ref[0], first)
        dec = jnp.exp(g)
        q_s[...] = _silu(q_ref[...]) * (HG_DK ** -0.5)
        o_ref[...] = jnp.zeros_like(o_ref)
        for h in range(HG_HEADS):
            hs = slice(h * HG_DK, (h + 1) * HG_DK)
            dec_t[hs, :] = dec[:, hs].T
            k_t[hs, :] = k[:, hs].T.astype(BF16)

    if zero_rest:
        so_ref[1:] = jnp.zeros((DEPTH - 1,) + so_ref.shape[1:], F32)

    lane = lax.broadcasted_iota(jnp.int32, (HG_DK, DEC_BATCH), 1)

    def body(j, carry):
        bidx = i * HG_STEP_NB + j
        m = lane == bidx
        for h in range(HG_HEADS):
            hs = slice(h * HG_DK, (h + 1) * HG_DK)
            d_col = _column(dec_t[hs, :], m)
            kv = _dot(k_t[hs, :], _one_hot_rows(_get_row(v_ref, bidx, hs), bidx))
            s_new = d_col * s_ref[0, j, h] + kv
            so_ref[0, j, h] = s_new
            o_row = _row_times(_get_row(q_s, bidx, hs), mat=s_new.astype(BF16))
            _set_row(o_ref, bidx, hs, _rms(o_row, nw_ref[0]) * _silu(_get_row(og_ref, bidx, hs)))
        return carry

    lax.fori_loop(0, HG_STEP_NB, body, 0)


def _stacked_state_specs(state, nb, l, out_state):
    tail = state.shape[2:]
    zeros = (0,) * len(tail)
    in_spec = pl.BlockSpec((1, nb) + tail, lambda i: (l, i) + zeros)
    if out_state is None:
        return in_spec, pl.BlockSpec((DEPTH, nb) + tail, lambda i: (0, i) + zeros), [], [], {}
    out_spec = pl.BlockSpec((1, nb) + tail, lambda i: (l, i) + zeros)
    return in_spec, out_spec, [pl.BlockSpec(memory_space=pl.ANY)], [out_state], None


def _hgrn_step(proj, state, lbs, hg_norm, out_state, l):
    nb = HG_STEP_NB
    blk = lambda c0: pl.BlockSpec((DEC_BATCH, D_MODEL), lambda i: (0, c0 // D_MODEL))
    sspec, ospec, extra_specs, extra_args, aliases = _stacked_state_specs(state, nb, l, out_state)
    in_specs = [
        blk(COL_Q), blk(COL_F), blk(COL_I), blk(COL_OG),
        pl.BlockSpec((1, 1, D_MODEL), lambda i: (l, 0, 0)),
        pl.BlockSpec((1, 1, HG_DV), lambda i: (l, 0, 0)),
        sspec,
    ] + extra_specs
    args = [proj, proj, proj, proj, lbs, hg_norm, state] + extra_args
    if aliases is None:
        aliases = {len(args) - 1: 1}

    def body(*refs):
        ins, rest = refs[:7], refs[len(args):]
        _hgrn_step_kernel(*ins, *rest, first=(l == 0), zero_rest=(out_state is None))

    return pl.pallas_call(
        body,
        grid=(DEC_BATCH // nb,),
        in_specs=in_specs,
        out_specs=[pl.BlockSpec((DEC_BATCH, D_MODEL), lambda i: (0, 0)), ospec],
        out_shape=[
            jax.ShapeDtypeStruct((DEC_BATCH, D_MODEL), F32),
            jax.ShapeDtypeStruct(state.shape, F32),
        ],
        scratch_shapes=[pltpu.VMEM((D_MODEL, DEC_BATCH), F32), pltpu.VMEM((D_MODEL, DEC_BATCH), BF16),
                        pltpu.VMEM((DEC_BATCH, D_MODEL), F32)],
        input_output_aliases=aliases,
        compiler_params=_cparams("arbitrary"),
        name="hgrn_step",
    )(*args)


def _ssm_prep_kernel(xs_ref, bm_ref, cm_ref, dt_ref, cst_ref, w_ref, cb_ref,
                     alog_ref, exp_ref,
                     xc_ref, bc_ref, cc_ref, dec_ref, xdtt_ref, cnew_ref):
    ch = SSM_CONV_CH
    parts = ((xs_ref, 0, SSM_INNER, xc_ref),
             (bm_ref, SSM_INNER, SSM_INNER + 1024, bc_ref),
             (cm_ref, SSM_INNER + 1024, ch, cc_ref))
    for u_ref, c0, c1, out_ref in parts:
        u = u_ref[...]
        acc = cb_ref[:, c0:c1] + w_ref[SSM_CONV - 1:SSM_CONV, c0:c1] * u
        for j in range(SSM_CONV - 1):
            acc = acc + w_ref[j:j + 1, c0:c1] * cst_ref[:, j * ch + c0:j * ch + c1]
        out_ref[...] = _silu(acc)
        cnew_ref[:, (SSM_CONV - 2) * ch + c0:(SSM_CONV - 2) * ch + c1] = u
    cnew_ref[:, 0:(SSM_CONV - 2) * ch] = cst_ref[:, ch:(SSM_CONV - 1) * ch]
    dtv = _dot_sel(dt_ref[...], exp_ref[...])
    dec_ref[...] = jnp.exp(dtv * -jnp.exp(alog_ref[...]))
    xdt = xc_ref[...] * dtv
    for m in range(SSM_INNER // LANE):
        ms = slice(m * LANE, (m + 1) * LANE)
        xdtt_ref[ms, :] = xdt[:, ms].T.astype(BF16)


def _ssm_prep(proj, dt, conv_state, p, l):
    n = DEC_BATCH
    full = lambda shape: pl.BlockSpec(shape, lambda i: tuple(0 for _ in shape))
    lspec = lambda rows, width: pl.BlockSpec((1, rows, width), lambda i: (l, 0, 0))

    def body(xs, bm, cm, dtr, cst, w, cb, alog, ex, *outs):
        _ssm_prep_kernel(xs, bm, cm, dtr, cst.at[0], w.at[0], cb.at[0], alog.at[0], ex, *outs)

    return pl.pallas_call(
        body,
        grid=(1,),
        in_specs=[
            pl.BlockSpec((n, SSM_INNER), lambda i: (0, COL_XBC // SSM_INNER)),
            pl.BlockSpec((n, 1024), lambda i: (0, (COL_XBC + SSM_INNER) // 1024)),
            pl.BlockSpec((n, 1024), lambda i: (0, (COL_XBC + SSM_INNER) // 1024 + 1)),
            full((n, DT_PAD)),
            pl.BlockSpec((1, n, (SSM_CONV - 1) * SSM_CONV_CH), lambda i: (l, 0, 0)),
            lspec(SSM_CONV, SSM_CONV_CH), lspec(1, SSM_CONV_CH),
            lspec(1, SSM_INNER),
            full((DT_PAD, SSM_INNER)),
        ],
        out_specs=[
            full((n, SSM_INNER)), full((n, 1024)), full((n, 1024)),
            full((n, SSM_INNER)), full((SSM_INNER, n)),
            full((n, (SSM_CONV - 1) * SSM_CONV_CH)),
        ],
        out_shape=[
            jax.ShapeDtypeStruct((n, SSM_INNER), F32),
            jax.ShapeDtypeStruct((n, 1024), F32),
            jax.ShapeDtypeStruct((n, 1024), F32),
            jax.ShapeDtypeStruct((n, SSM_INNER), F32),
            jax.ShapeDtypeStruct((SSM_INNER, n), BF16),
            jax.ShapeDtypeStruct((n, (SSM_CONV - 1) * SSM_CONV_CH), F32),
        ],
        compiler_params=_cparams("arbitrary"),
        name="ssm_prep",
    )(proj, proj, proj, dt, conv_state, p['conv_w'], p['conv_b'], p['a_log_x'], p['expand_all'])


SSM_STEP_NB = 4


def _ssm_step_kernel(dec_ref, xdtt_ref, bc_ref, cc_ref, s_ref, y_ref, so_ref, *, zero_rest):
    i = pl.program_id(0)

    @pl.when(i == 0)
    def _():
        y_ref[...] = jnp.zeros_like(y_ref)

    if zero_rest:
        so_ref[1:] = jnp.zeros((DEPTH - 1,) + so_ref.shape[1:], F32)

    def body(j, carry):
        bidx = i * SSM_STEP_NB + j
        for g in range(SSM_GROUPS):
            rs = slice(g * GROUP_W, (g + 1) * GROUP_W)
            hs = slice(g * SSM_HPG, (g + 1) * SSM_HPG)
            ns = slice(g * SSM_STATE, (g + 1) * SSM_STATE)
            dec_row = _get_row(dec_ref, bidx, rs)
            decay = jnp.concatenate(
                [jnp.broadcast_to(dec_row[:, k * SSM_HEADDIM:k * SSM_HEADDIM + 1],
                                  (SSM_HEADDIM, SSM_STATE)) for k in range(SSM_HPG)], axis=0)
            inc = _dot(xdtt_ref[rs, :], _one_hot_rows(_get_row(bc_ref, bidx, ns), bidx))
            h = s_ref[0, j, hs].reshape(GROUP_W, SSM_STATE)
            h_new = decay * h + inc
            so_ref[0, j, hs] = h_new.reshape(SSM_HPG, SSM_HEADDIM, SSM_STATE)
            y_row = _row_times(_get_row(cc_ref, bidx, ns), mat_nt=h_new.astype(BF16))
            _set_row(y_ref, bidx, rs, y_row)
        return carry

    lax.fori_loop(0, SSM_STEP_NB, body, 0)


def _ssm_step(dec, xdtt, bc, cc, state, out_state, l):
    nb = SSM_STEP_NB
    full = lambda shape: pl.BlockSpec(shape, lambda i: tuple(0 for _ in shape))
    sspec, ospec, extra_specs, extra_args, aliases = _stacked_state_specs(state, nb, l, out_state)
    in_specs = [full(dec.shape), full(xdtt.shape), full(bc.shape), full(cc.shape),
                sspec] + extra_specs
    args = [dec, xdtt, bc, cc, state] + extra_args
    if aliases is None:
        aliases = {len(args) - 1: 1}

    def body(*refs):
        ins, rest = refs[:5], refs[len(args):]
        _ssm_step_kernel(*ins, *rest, zero_rest=(out_state is None))

    return pl.pallas_call(
        body,
        grid=(DEC_BATCH // nb,),
        in_specs=in_specs,
        out_specs=[full((DEC_BATCH, SSM_INNER)), ospec],
        out_shape=[
            jax.ShapeDtypeStruct((DEC_BATCH, SSM_INNER), F32),
            jax.ShapeDtypeStruct(state.shape, F32),
        ],
        input_output_aliases=aliases,
        compiler_params=_cparams("arbitrary"),
        name="ssm_step",
    )(*args)


def _ssm_post_kernel(y_in_ref, xc_ref, z_ref, dsk_ref, nw_ref, y_ref):
    for g in range(SSM_GROUPS):
        gs = slice(g * GROUP_W, (g + 1) * GROUP_W)
        y = (y_in_ref[:, gs] + dsk_ref[:, gs] * xc_ref[:, gs]) * _silu(z_ref[:, gs])
        y_ref[:, gs] = _rms(y, nw_ref[:, gs])


def _ssm_post(y, xc, proj, p, l):
    n = DEC_BATCH
    full = lambda shape: pl.BlockSpec(shape, lambda i: tuple(0 for _ in shape))
    lspec = pl.BlockSpec((1, 1, SSM_INNER), lambda i: (l, 0, 0))

    def body(y_in, xc_r, z_r, dsk, nw, y_r):
        _ssm_post_kernel(y_in, xc_r, z_r, dsk.at[0], nw.at[0], y_r)

    return pl.pallas_call(
        body,
        grid=(1,),
        in_specs=[full((n, SSM_INNER)), full((n, SSM_INNER)),
                  pl.BlockSpec((n, SSM_INNER), lambda i: (0, COL_Z // SSM_INNER)),
                  lspec, lspec],
        out_specs=full((n, SSM_INNER)),
        out_shape=jax.ShapeDtypeStruct((n, SSM_INNER), F32),
        compiler_params=_cparams("arbitrary"),
        name="ssm_post",
    )(y, xc, proj, p['d_skip_x'], p['ssm_norm'])


def _merge_kernel(x_ref, oh_ref, y_ref, ga_ref, gb_ref, g1_ref, bma_ref, bmb_ref,
                  wa_ref, wb_ref, wo_ref, o_ref):
    a = _dot(oh_ref[...].astype(BF16), wa_ref[0])
    b = _dot(y_ref[...].astype(BF16), wb_ref[0])
    u = _sigmoid(ga_ref[...] + bma_ref[0]) * a + _sigmoid(gb_ref[...] + bmb_ref[0]) * b
    o_ref[...] = x_ref[...] + g1_ref[0] * _dot(u.astype(BF16), wo_ref[0])


def _merge(x, o_hg, y, proj, mod, p, l, tm, rows_per_group):
    rows = x.shape[0]
    wspec = lambda k: pl.BlockSpec((1, k, D_MODEL), lambda i: (l, 0, 0))
    return pl.pallas_call(
        _merge_kernel,
        grid=(rows // tm,),
        in_specs=[
            pl.BlockSpec((tm, D_MODEL), lambda i: (i, 0)),
            pl.BlockSpec((tm, D_MODEL), lambda i: (i, 0)),
            pl.BlockSpec((tm, SSM_INNER), lambda i: (i, 0)),
            pl.BlockSpec((tm, D_MODEL), lambda i: (i, COL_GA // D_MODEL)),
            pl.BlockSpec((tm, D_MODEL), lambda i: (i, COL_GB // D_MODEL)),
            _mod_spec(mod, tm, rows_per_group, ADA_G1),
            pl.BlockSpec((1, 1, D_MODEL), lambda i: (l, 0, 0)),
            pl.BlockSpec((1, 1, D_MODEL), lambda i: (l, 0, 1)),
            wspec(D_MODEL), wspec(SSM_INNER), wspec(D_MODEL),
        ],
        out_specs=pl.BlockSpec((tm, D_MODEL), lambda i: (i, 0)),
        out_shape=jax.ShapeDtypeStruct((rows, D_MODEL), F32),
        compiler_params=_cparams("arbitrary"),
        name="merge",
    )(x, o_hg, y, proj, proj, mod, p['b_merge'], p['b_merge'], p['w_br_a'], p['w_br_b'],
      p['w_out'])


FF_CHUNK = 1024


def _mlp_kernel(x_ref, sc_ref, sh_ref, g_ref, nw_ref, wu_ref, wd_ref, fw_ref, o_ref, *, final):
    x = x_ref[...]
    h = (_rms(x, nw_ref[0]) * (1.0 + sc_ref[0]) + sh_ref[0]).astype(BF16)
    acc = jnp.zeros(x.shape, F32)
    for c in range(D_FF // FF_CHUNK):
        cs = slice(c * FF_CHUNK, (c + 1) * FF_CHUNK)
        a = jnp.maximum(_dot(h, wu_ref[0, :, cs]), 0.0)
        acc = acc + _dot((a * a).astype(BF16), wd_ref[0, cs, :])
    out = x + g_ref[0] * acc
    if final:
        out = _rms(out, fw_ref[...])
    o_ref[...] = out


def _mlp(x, mod, p, norm_final, l, tm, rows_per_group):
    rows = x.shape[0]
    return pl.pallas_call(
        functools.partial(_mlp_kernel, final=(l == DEPTH - 1)),
        grid=(rows // tm,),
        in_specs=[
            pl.BlockSpec((tm, D_MODEL), lambda i: (i, 0)),
            _mod_spec(mod, tm, rows_per_group, ADA_SC2),
            _mod_spec(mod, tm, rows_per_group, ADA_SH2),
            _mod_spec(mod, tm, rows_per_group, ADA_G2),
            pl.BlockSpec((1, 1, D_MODEL), lambda i: (l, 0, 0)),
            pl.BlockSpec((1, D_MODEL, D_FF), lambda i: (l, 0, 0)),
            pl.BlockSpec((1, D_FF, D_MODEL), lambda i: (l, 0, 0)),
            pl.BlockSpec((1, D_MODEL), lambda i: (0, 0)),
        ],
        out_specs=pl.BlockSpec((tm, D_MODEL), lambda i: (i, 0)),
        out_shape=jax.ShapeDtypeStruct((rows, D_MODEL), F32),
        compiler_params=_cparams("arbitrary"),
        name="mlp",
    )(x, mod, mod, mod, p['norm_mlp'], p['w_up'], p['w_down'], norm_final)


def _prepare_params(w_ada, b_ada, norm_mix, w_in, b_merge, hg_norm, conv_w, conv_b, dt_bias,
                    a_log, d_skip, ssm_norm, w_br_a, w_br_b, w_out, norm_mlp, w_up, w_down):
    dt0 = COL_XBC + SSM_CONV_CH
    w_gate = w_in[:, :, dt0 + SSM_HEADS:]
    w_dt = jnp.pad(w_in[:, :, dt0:dt0 + SSM_HEADS], ((0, 0), (0, 0), (0, DT_PAD - SSM_HEADS)))
    per_head = lambda a: jnp.repeat(a, SSM_HEADDIM, axis=-1).reshape(DEPTH, 1, SSM_INNER)
    narrow = lambda a: jnp.pad(a, ((0, 0), (0, DT_PAD - SSM_HEADS))).reshape(DEPTH, 1, DT_PAD)
    head_of_col = jnp.arange(SSM_INNER) // SSM_HEADDIM
    expand_all = (jnp.arange(DT_PAD)[:, None] == head_of_col[None, :]).astype(BF16)
    expand_g = expand_all.reshape(DT_PAD, SSM_GROUPS, GROUP_W).transpose(1, 0, 2)
    return {
        'norm_mix': norm_mix.reshape(DEPTH, 1, D_MODEL),
        'w_in': w_in.astype(BF16), 'w_gate': w_gate.astype(BF16), 'w_dt': w_dt.astype(BF16),
        'b_merge': b_merge.reshape(DEPTH, 1, 2 * D_MODEL),
        'hg_norm': hg_norm.reshape(DEPTH, 1, HG_DV),
        'conv_w': conv_w, 'conv_b': conv_b.reshape(DEPTH, 1, SSM_CONV_CH),
        'dt_bias_n': narrow(dt_bias), 'a_log_n': narrow(a_log),
        'a_log_x': per_head(a_log), 'd_skip_x': per_head(d_skip),
        'ssm_norm': ssm_norm.reshape(DEPTH, 1, SSM_INNER),
        'expand_all': expand_all, 'expand_g': expand_g,
        'w_br_a': w_br_a.astype(BF16), 'w_br_b': w_br_b.astype(BF16),
        'w_out': w_out.astype(BF16),
        'norm_mlp': norm_mlp.reshape(DEPTH, 1, D_MODEL),
        'w_up': w_up.astype(BF16), 'w_down': w_down.astype(BF16),
    }


def kernel(x_prompt, x_sample, state_hgrn, state_ssm, state_conv, c_prompt, c_sample, w_ada, b_ada, norm_mix, w_in, b_merge, lower_bounds, hg_norm, conv_w, conv_b, dt_bias, a_log, d_skip, ssm_norm, w_br_a, w_br_b, w_out, norm_mlp, w_up, w_down, norm_final):
    p = _prepare_params(w_ada, b_ada, norm_mix, w_in, b_merge, hg_norm, conv_w, conv_b, dt_bias,
                        a_log, d_skip, ssm_norm, w_br_a, w_br_b, w_out, norm_mlp, w_up, w_down)
    lbs = _lower_bounds(lower_bounds).reshape(DEPTH, 1, HG_HEADS * HG_DK)
    mod = _modulation(jnp.concatenate([c_prompt, c_sample], axis=0), w_ada, b_ada)
    mod_p = mod[:, :BATCH].reshape(DEPTH, BATCH, 1, N_ADA * D_MODEL)
    mod_s = mod[:, BATCH:].reshape(DEPTH, 1, DEC_BATCH, N_ADA * D_MODEL)
    norm_final2 = norm_final.reshape(1, D_MODEL)
    conv_state = state_conv.reshape(DEPTH, DEC_BATCH, (SSM_CONV - 1) * SSM_CONV_CH)

    xp = x_prompt.reshape(BATCH * SEQ, D_MODEL)
    xs = x_sample.reshape(DEC_BATCH, D_MODEL)
    hg_p, ssm_p, cv_p, cv_s = [], [], [], []
    hg_s = ssm_s = None
    for l in range(DEPTH):
        proj, dt, cum = _inproj(xp, mod_p[l], p, l, SEQ, SEQ)
        o_hg, s_new, y, h_new = _mixer_prompt(proj, dt, cum, lbs, p, l)
        hg_p.append(s_new)
        ssm_p.append(h_new)
        cv_p.append(proj.reshape(BATCH, SEQ, PROJ_W)[:, SEQ - (SSM_CONV - 1):,
                                                       COL_XBC:COL_XBC + SSM_CONV_CH])
        xp = _merge(xp, o_hg, y, proj, mod_p[l], p, l, 512, SEQ)
        xp = _mlp(xp, mod_p[l], p, norm_final2, l, 512, SEQ)
        proj, dt, _ = _inproj(xs, mod_s[l], p, l, DEC_BATCH, DEC_BATCH)
        o_hg, hg_s = _hgrn_step(proj, state_hgrn, lbs, p['hg_norm'], hg_s, l)
        xc, bc, cc, dect, xdtt, cnew = _ssm_prep(proj, dt, conv_state, p, l)
        yt, ssm_s = _ssm_step(dect, xdtt, bc, cc, state_ssm, ssm_s, l)
        y = _ssm_post(yt, xc, proj, p, l)
        cv_s.append(cnew.reshape(DEC_BATCH, SSM_CONV - 1, SSM_CONV_CH))
        xs = _merge(xs, o_hg, y, proj, mod_s[l], p, l, DEC_BATCH, DEC_BATCH)
        xs = _mlp(xs, mod_s[l], p, norm_final2, l, DEC_BATCH, DEC_BATCH)
    return (xp.reshape(BATCH, SEQ, D_MODEL), xs.reshape(DEC_BATCH, 1, D_MODEL),
            jnp.stack(hg_p), jnp.stack(ssm_p), jnp.stack(cv_p),
            hg_s, ssm_s, jnp.stack(cv_s))
```

```python
import functools

import jax
import jax.numpy as jnp
import numpy as np
from jax import lax
from jax.experimental import pallas as pl
from jax.experimental.pallas import tpu as pltpu

F32 = jnp.float32
BF16 = jnp.bfloat16

D_MODEL = 1024
BATCH = 8
SEQ = 2048
DEPTH = 4
DEC_BATCH = 128
HG_HEADS = 8
HG_DK = 128
HG_DV = 128
SSM_INNER = 2048
SSM_HEADDIM = 64
SSM_HEADS = 32
SSM_GROUPS = 8
SSM_HPG = 4
SSM_STATE = 128
SSM_CONV = 4
SSM_CONV_CH = 4096
GROUP_W = SSM_HPG * SSM_HEADDIM
D_FF = 4096
N_ADA = 6
EPS = 1e-6
DT_PAD = 128
COL_Q, COL_F, COL_I, COL_OG = 0, 1024, 2048, 3072
COL_Z, COL_XBC, COL_GA, COL_GB = 4096, 6144, 10240, 11264
PROJ_W = 12288
ADA_SH1, ADA_SC1, ADA_G1, ADA_SH2, ADA_SC2, ADA_G2 = range(6)

LANE = 128
SUBLANES = 8
TILE = 128
VMEM_LIMIT = 56 * 1024 * 1024


def _cparams(*sem):
    return pltpu.CompilerParams(dimension_semantics=sem, vmem_limit_bytes=VMEM_LIMIT)


def _sigmoid(x):
    return 0.5 * jnp.tanh(0.5 * x) + 0.5


def _silu(x):
    return x * _sigmoid(x)


def _softplus(x):
    return jnp.maximum(x, 0.0) + jnp.log1p(jnp.exp(-jnp.abs(x)))


def _dot(a, b):
    return jnp.dot(a, b, preferred_element_type=F32)


def _dot_nt(a, b):
    return lax.dot_general(a, b, (((1,), (1,)), ((), ())), preferred_element_type=F32)


def _split(a, terms):
    parts = []
    for _ in range(terms - 1):
        hi = a.astype(BF16)
        parts.append(hi)
        a = a - hi.astype(F32)
    parts.append(a.astype(BF16))
    return parts


def _sel_dot(sel, a, terms=3):
    return _dot(jnp.concatenate([sel] * terms, axis=1), jnp.concatenate(_split(a, terms), axis=0))


def _dot_sel(a, sel, terms=3):
    return _dot(jnp.concatenate(_split(a, terms), axis=1), jnp.concatenate([sel] * terms, axis=0))


def _rms(x, w):
    ms = jnp.mean(x * x, axis=-1, keepdims=True)
    return x * lax.rsqrt(ms + EPS) * w


def _hgrn_gates(f, lb, first):
    if first:
        g = jnp.minimum(f, 0.0) - jnp.log1p(jnp.exp(-jnp.abs(f)))
        k = _sigmoid(-f)
    else:
        fg = lb + (1.0 - lb) * _sigmoid(f)
        g = jnp.log(fg)
        k = 1.0 - fg
    return g, k


def _lbs_kernel(lb_ref, o_ref):
    x = lb_ref[...]
    m = jnp.max(x, axis=0, keepdims=True)
    e = jnp.exp(x - m)
    p = e / jnp.sum(e, axis=0, keepdims=True)
    acc = jnp.zeros_like(p[0:1])
    rows = [acc]
    for l in range(1, DEPTH):
        acc = acc + p[l:l + 1]
        rows.append(acc)
    o_ref[...] = jnp.concatenate(rows, axis=0)


def _lower_bounds(lower_bounds):
    return pl.pallas_call(
        _lbs_kernel, out_shape=jax.ShapeDtypeStruct(lower_bounds.shape, F32), name="lbs",
    )(lower_bounds)


MOD_STREAMS = 3


def _mod_kernel(c_ref, *refs):
    w_refs, b_ref, o_ref = refs[:MOD_STREAMS], refs[MOD_STREAMS], refs[MOD_STREAMS + 1]
    a = _silu(c_ref[...]).astype(BF16)
    for k, w_ref in enumerate(w_refs):
        cs = slice(k * D_MODEL, (k + 1) * D_MODEL)
        o_ref[0, :, cs] = _dot(a, w_ref[0].astype(BF16)) + b_ref[0, :, cs]


def _modulation(c_all, w_ada, b_ada):
    n = c_all.shape[0]
    tn = MOD_STREAMS * D_MODEL
    w_spec = lambda k: pl.BlockSpec((1, D_MODEL, D_MODEL), lambda l, j: (l, 0, j * MOD_STREAMS + k))
    return pl.pallas_call(
        _mod_kernel,
        grid=(DEPTH, N_ADA * D_MODEL // tn),
        in_specs=[pl.BlockSpec((n, D_MODEL), lambda l, j: (0, 0))]
        + [w_spec(k) for k in range(MOD_STREAMS)]
        + [pl.BlockSpec((1, 1, tn), lambda l, j: (l, 0, j))],
        out_specs=pl.BlockSpec((1, n, tn), lambda l, j: (l, 0, j)),
        out_shape=jax.ShapeDtypeStruct((DEPTH, n, N_ADA * D_MODEL), F32),
        compiler_params=_cparams("arbitrary", "arbitrary"),
        name="modulation",
    )(c_all, *([w_ada] * MOD_STREAMS), b_ada.reshape(DEPTH, 1, N_ADA * D_MODEL))


def _mod_spec(mod, tm, rows_per_group, piece):
    if mod.shape[1] == 1:
        tiles = rows_per_group // tm
        return pl.BlockSpec((1, 1, D_MODEL), lambda i, *_: (i // tiles, 0, piece))
    return pl.BlockSpec((1, tm, D_MODEL), lambda i, *_: (0, i, piece))


def _inproj_kernel(x_ref, sc_ref, sh_ref, nw_ref, *refs, streams):
    w_refs = refs[:streams]
    wdt_ref, dtb_ref, alog_ref, o_ref, odt_ref, ocum_ref, h_scr = refs[streams:]

    @pl.when(pl.program_id(1) == 0)
    def _():
        h = _rms(x_ref[...], nw_ref[0]) * (1.0 + sc_ref[0]) + sh_ref[0]
        hb = h.astype(BF16)
        h_scr[...] = hb
        dtv = _softplus(_dot(hb, wdt_ref[0]) + dtb_ref[0])
        odt_ref[...] = dtv
        la = dtv * -jnp.exp(alog_ref[0])
        row = lax.broadcasted_iota(jnp.int32, (TILE, TILE), 0)
        col = lax.broadcasted_iota(jnp.int32, (TILE, TILE), 1)
        tri = jnp.where(row >= col, 1.0, 0.0).astype(BF16)
        for i in range(la.shape[0] // TILE):
            rs = slice(i * TILE, (i + 1) * TILE)
            ocum_ref[rs, :] = _sel_dot(tri, la[rs] * LOG2E)

    h = h_scr[...]
    for k, w_ref in enumerate(w_refs):
        o_ref[:, k * INPROJ_TN:(k + 1) * INPROJ_TN] = _dot(h, w_ref[0])


INPROJ_TN = 1024


def _inproj(x, mod, p, l, tm, rows_per_group, streams):
    rows = x.shape[0]
    tn = streams * INPROJ_TN
    narrow = pl.BlockSpec((1, 1, DT_PAD), lambda i, j: (l, 0, 0))
    w_spec = lambda k: pl.BlockSpec((1, D_MODEL, INPROJ_TN), lambda i, j: (l, 0, j * streams + k))
    return pl.pallas_call(
        functools.partial(_inproj_kernel, streams=streams),
        grid=(rows // tm, PROJ_W // tn),
        in_specs=[
            pl.BlockSpec((tm, D_MODEL), lambda i, j: (i, 0)),
            _mod_spec(mod, tm, rows_per_group, ADA_SC1),
            _mod_spec(mod, tm, rows_per_group, ADA_SH1),
            pl.BlockSpec((1, 1, D_MODEL), lambda i, j: (l, 0, 0)),
        ] + [w_spec(k) for k in range(streams)] + [
            pl.BlockSpec((1, D_MODEL, DT_PAD), lambda i, j: (l, 0, 0)),
            narrow, narrow,
        ],
        out_specs=[
            pl.BlockSpec((tm, tn), lambda i, j: (i, j)),
            pl.BlockSpec((tm, DT_PAD), lambda i, j: (i, 0)),
            pl.BlockSpec((tm, DT_PAD), lambda i, j: (i, 0)),
        ],
        out_shape=[
            jax.ShapeDtypeStruct((rows, PROJ_W), F32),
            jax.ShapeDtypeStruct((rows, DT_PAD), F32),
            jax.ShapeDtypeStruct((rows, DT_PAD), F32),
        ],
        scratch_shapes=[pltpu.VMEM((tm, D_MODEL), BF16)],
        compiler_params=_cparams("arbitrary", "arbitrary"),
        name="inproj",
    )(x, mod, mod, p['norm_mix'], *([p['w_main']] * streams), p['w_dt'], p['dt_bias_n'],
      p['a_log_n'])


HG_LEVELS = 7
HG_HB = 4
HG_SPLIT = 2
LOG2E = 1.4426950408889634


def _hgrn_sum_ranges():
    r = np.arange(TILE)[:, None]
    c = np.arange(TILE)[None, :]
    mats = [c <= r, c > r]
    for s in range(HG_LEVELS):
        h = 1 << s
        right = ((r >> s) & 1) == 1
        in_right = (c >= (r & ~(h - 1))) & (c <= r)
        in_left = (c > r) & (c <= (r | (h - 1)))
        mats.append(np.where(right, in_right, in_left))
    m = np.concatenate(mats, axis=0).astype(np.float32)
    return np.tile(m, (1, HG_SPLIT))


def _hgrn_level_masks():
    r = np.arange(TILE)[:, None]
    c = np.arange(TILE)[None, :]
    x = r ^ c
    top = np.where(x > 0, np.floor(np.log2(np.maximum(x, 1))).astype(np.int64), -1)
    masks = [r == c] + [(r > c) & (top == s) for s in range(HG_LEVELS)]
    return np.stack(masks).astype(np.float32)


def _hgrn_tile(rows, q_ref, f_ref, v_ref, og_ref, lb, nw, sums_ref, mask_ref, o_ref, st_scr, first):
    g, k = _hgrn_gates(f_ref[rows, :], lb, first)
    g2 = jnp.concatenate(_split(g * LOG2E, HG_SPLIT), axis=0)
    decay = jnp.exp2(_dot(sums_ref[...], g2))
    q = _silu(q_ref[rows, :]) * (HG_DK ** -0.5)
    v = v_ref[rows, :]
    og = _silu(og_ref[rows, :])
    q_in = (q * decay[0:TILE]).astype(BF16)
    k_out = (k * decay[TILE:2 * TILE]).astype(BF16)
    end = decay[TILE - 1:TILE]
    qb = q.astype(BF16)
    kb = k.astype(BF16)
    vb = v.astype(BF16)
    outs = []
    for hh in range(HG_HB):
        cs = slice(hh * HG_DK, (hh + 1) * HG_DK)
        st = st_scr[hh]
        p = mask_ref[0] * _dot_nt(qb[:, cs], kb[:, cs])
        for s in range(HG_LEVELS):
            a = decay[(2 + s) * TILE:(3 + s) * TILE, cs].astype(BF16)
            p = p + mask_ref[1 + s] * _dot_nt(qb[:, cs] * a, kb[:, cs] * a)
        o = _dot_nt(q_in[:, cs], st.astype(BF16)) + _dot(p.astype(BF16), vb[:, cs])
        st_scr[hh] = st * end[:, cs] + _dot(v[:, cs].T.astype(BF16), k_out[:, cs])
        outs.append(_rms(o, nw) * og[:, cs])
    o_ref[rows, :] = jnp.concatenate(outs, axis=1).astype(o_ref.dtype)


def _conv_silu_tile(ref, cols, w_ref, b_ref, tail, r0, t):
    cur = ref[pl.ds(r0, TILE), cols]
    prev = ref[pl.ds(pl.multiple_of(jnp.maximum(r0 - SUBLANES, 0), SUBLANES), SUBLANES), cols]
    prev = jnp.where(t > 0, prev, tail)
    ext = jnp.concatenate([prev, cur], axis=0)
    acc = b_ref[:, cols] + w_ref[SSM_CONV - 1:SSM_CONV, cols] * cur
    for j in range(SSM_CONV - 1):
        acc = acc + w_ref[j:j + 1, cols] * ext[5 + j:5 + j + TILE, :]
    return _silu(acc)


def _ssd_tile(rows, r0, t, gi, x_ref, bm_ref, cm_ref, z_ref, dt_ref, cum_ref,
              wx_ref, wb_ref, wc_ref, cbx_ref, cbb_ref, cbc_ref, dsk_ref, nw_ref, exp_ref,
              tail_scr, y_ref, ht_scr, causal, head_of_lane):
    xs = slice(gi * GROUP_W, (gi + 1) * GROUP_W)
    ns = slice(gi * SSM_STATE, (gi + 1) * SSM_STATE)
    b_off = MIX_GROUPS * GROUP_W
    c_off = b_off + MIX_GROUPS * SSM_STATE
    tail_b = slice(b_off + gi * SSM_STATE, b_off + (gi + 1) * SSM_STATE)
    tail_c = slice(c_off + gi * SSM_STATE, c_off + (gi + 1) * SSM_STATE)
    xc = _conv_silu_tile(x_ref, xs, wx_ref, cbx_ref, tail_scr[:, xs], r0, t)
    bc = _conv_silu_tile(bm_ref, ns, wb_ref, cbb_ref, tail_scr[:, tail_b], r0, t)
    cc = _conv_silu_tile(cm_ref, ns, wc_ref, cbc_ref, tail_scr[:, tail_c], r0, t)
    expand = exp_ref[gi]
    dtv = _dot_sel(dt_ref[rows, :], expand, terms=2)
    cum = _dot_sel(cum_ref[rows, :], expand)
    cum_end = cum[TILE - 1:TILE, :]
    cum_t = cum.T
    xdt = xc * dtv
    xdtb = xdt.astype(BF16)
    ccb = cc.astype(BF16)
    cb = jnp.where(causal, _dot_nt(ccb, bc.astype(BF16)), 0.0)
    ht = ht_scr[gi]
    weights, values = [], []
    for k in range(SSM_HPG):
        c_col = cum[:, k * SSM_HEADDIM:k * SSM_HEADDIM + 1]
        c_row = cum_t[k * SSM_HEADDIM:k * SSM_HEADDIM + 1, :]
        decay = jnp.exp2(jnp.minimum(c_col - c_row, 0.0))
        weights.append((cb * decay).astype(BF16))
        values.append(jnp.where(head_of_lane == k, xdtb, jnp.zeros_like(xdtb)))
    y = (_dot(ccb, ht.astype(BF16)) * jnp.exp2(cum)
         + _dot(jnp.concatenate(weights, axis=1), jnp.concatenate(values, axis=0)))
    x_out = (xdt * jnp.exp2(cum_end - cum)).astype(BF16)
    ht_scr[gi] = ht * jnp.exp2(cum_end) + _dot(bc.T.astype(BF16), x_out)
    y = (y + dsk_ref[:, xs] * xc) * _silu(z_ref[rows, xs])
    y_ref[rows, xs] = _rms(y, nw_ref[:, xs]).astype(y_ref.dtype)


MIX_ROWS = 512
MIX_GROUPS = SSM_GROUPS // (HG_HEADS // HG_HB)


def _mixer_prompt_kernel(q_ref, f_ref, v_ref, og_ref, lb_ref, hnw_ref, sums_ref, mask_ref,
                         x_ref, bm_ref, cm_ref, z_ref, dt_ref, cum_ref,
                         wx_ref, wb_ref, wc_ref, cbx_ref, cbb_ref, cbc_ref,
                         dsk_ref, snw_ref, exp_ref,
                         o_ref, s_ref, y_ref, h_ref, st_scr, ht_scr, tail_scr, *, first):
    chunk = pl.program_id(2)

    @pl.when(chunk == 0)
    def _():
        st_scr[...] = jnp.zeros_like(st_scr)
        ht_scr[...] = jnp.zeros_like(ht_scr)
        tail_scr[...] = jnp.zeros_like(tail_scr)

    row = lax.broadcasted_iota(jnp.int32, (TILE, TILE), 0)
    col = lax.broadcasted_iota(jnp.int32, (TILE, TILE), 1)
    causal = row >= col
    head_of_lane = lax.broadcasted_iota(jnp.int32, (TILE, GROUP_W), 1) // SSM_HEADDIM
    lb = lb_ref[0]
    hnw = hnw_ref[0]

    def tile_body(t, carry):
        r0 = pl.multiple_of(t * TILE, TILE)
        rows = pl.ds(r0, TILE)
        _hgrn_tile(rows, q_ref, f_ref, v_ref, og_ref, lb, hnw, sums_ref, mask_ref, o_ref, st_scr,
                   first)
        for gi in range(MIX_GROUPS):
            _ssd_tile(rows, r0, t, gi, x_ref, bm_ref, cm_ref, z_ref, dt_ref, cum_ref,
                      wx_ref, wb_ref, wc_ref, cbx_ref, cbb_ref, cbc_ref, dsk_ref, snw_ref, exp_ref,
                      tail_scr, y_ref, ht_scr, causal, head_of_lane)
        return carry

    lax.fori_loop(0, MIX_ROWS // TILE, tile_body, 0, unroll=2)

    last = slice(MIX_ROWS - SUBLANES, MIX_ROWS)
    b_off = MIX_GROUPS * GROUP_W
    c_off = b_off + MIX_GROUPS * SSM_STATE
    tail_scr[:, 0:b_off] = x_ref[last, :]
    tail_scr[:, b_off:c_off] = bm_ref[last, :]
    tail_scr[:, c_off:] = cm_ref[last, :]

    @pl.when(chunk == pl.num_programs(2) - 1)
    def _():
        for hh in range(HG_HB):
            s_ref[0, hh] = st_scr[hh].T
        for gi in range(MIX_GROUPS):
            h_ref[0, gi * SSM_HPG:(gi + 1) * SSM_HPG] = ht_scr[gi].T.reshape(
                SSM_HPG, SSM_HEADDIM, SSM_STATE)


def _mixer_prompt(proj, dt, cum, lbs, p, l):
    rows = MIX_ROWS
    n_chunks = SEQ // rows
    hw = HG_HB * HG_DK
    xw = MIX_GROUPS * GROUP_W
    nw = MIX_GROUPS * SSM_STATE
    b0 = COL_XBC + SSM_INNER
    c0 = b0 + SSM_GROUPS * SSM_STATE
    act = lambda width, col0: pl.BlockSpec(
        (rows, width), lambda b, h, c: (b * n_chunks + c, col0 // width + h))
    par = lambda nrows, width, col0: pl.BlockSpec(
        (1, nrows, width), lambda b, h, c: (l, 0, col0 // width + h))
    narrow = pl.BlockSpec((rows, DT_PAD), lambda b, h, c: (b * n_chunks + c, 0))
    sums = jnp.asarray(_hgrn_sum_ranges(), BF16)
    masks = jnp.asarray(_hgrn_level_masks(), F32)
    in_specs = [
        act(hw, COL_Q), act(hw, COL_F), act(hw, COL_I), act(hw, COL_OG),
        par(1, hw, 0), pl.BlockSpec((1, 1, HG_DV), lambda b, h, c: (l, 0, 0)),
        pl.BlockSpec(sums.shape, lambda b, h, c: (0, 0)),
        pl.BlockSpec(masks.shape, lambda b, h, c: (0, 0, 0)),
        act(xw, COL_XBC), act(nw, b0), act(nw, c0), act(xw, COL_Z), narrow, narrow,
        par(SSM_CONV, xw, 0), par(SSM_CONV, nw, SSM_INNER),
        par(SSM_CONV, nw, SSM_INNER + SSM_GROUPS * SSM_STATE),
        par(1, xw, 0), par(1, nw, SSM_INNER), par(1, nw, SSM_INNER + SSM_GROUPS * SSM_STATE),
        par(1, xw, 0), par(1, xw, 0),
        pl.BlockSpec((MIX_GROUPS, DT_PAD, GROUP_W), lambda b, h, c: (h, 0, 0)),
    ]
    args = (proj, proj, proj, proj, lbs, p['hg_norm'], sums, masks,
            proj, proj, proj, proj, dt, cum,
            p['conv_w'], p['conv_w'], p['conv_w'], p['conv_b'], p['conv_b'], p['conv_b'],
            p['d_skip_x'], p['ssm_norm'], p['expand_g'])

    def body(*refs):
        ins = list(refs[:len(args)])
        for i in range(14, 22):
            ins[i] = ins[i].at[0]
        _mixer_prompt_kernel(*ins, *refs[len(args):], first=(l == 0))

    return pl.pallas_call(
        body,
        grid=(BATCH, HG_HEADS // HG_HB, n_chunks),
        in_specs=in_specs,
        out_specs=[
            pl.BlockSpec((rows, hw), lambda b, h, c: (b * n_chunks + c, h)),
            pl.BlockSpec((1, HG_HB, HG_DK, HG_DV), lambda b, h, c: (b, h, 0, 0)),
            pl.BlockSpec((rows, xw), lambda b, h, c: (b * n_chunks + c, h)),
            pl.BlockSpec((1, MIX_GROUPS * SSM_HPG, SSM_HEADDIM, SSM_STATE),
                         lambda b, h, c: (b, h, 0, 0)),
        ],
        out_shape=[
            jax.ShapeDtypeStruct((BATCH * SEQ, HG_HEADS * HG_DV), BF16),
            jax.ShapeDtypeStruct((BATCH, HG_HEADS, HG_DK, HG_DV), F32),
            jax.ShapeDtypeStruct((BATCH * SEQ, SSM_INNER), BF16),
            jax.ShapeDtypeStruct((BATCH, SSM_HEADS, SSM_HEADDIM, SSM_STATE), F32),
        ],
        scratch_shapes=[
            pltpu.VMEM((HG_HB, HG_DV, HG_DK), F32),
            pltpu.VMEM((MIX_GROUPS, SSM_STATE, GROUP_W), F32),
            pltpu.VMEM((SUBLANES, xw + 2 * nw), F32),
        ],
        compiler_params=_cparams("arbitrary", "arbitrary", "arbitrary"),
        name="mixer_prompt",
    )(*args)


HG_STEP_NB = 8


def _column(mat_t, lane_mask):
    return jnp.sum(jnp.where(lane_mask, mat_t, 0.0), axis=1, keepdims=True)


def _row_group(ref, bidx, cols):
    start = pl.multiple_of((bidx // SUBLANES) * SUBLANES, SUBLANES)
    idx = (pl.ds(start, SUBLANES), cols)
    blk = ref[idx]
    is_row = lax.broadcasted_iota(jnp.int32, blk.shape, 0) == bidx % SUBLANES
    return idx, blk, is_row


def _get_row(ref, bidx, cols):
    _, blk, is_row = _row_group(ref, bidx, cols)
    return jnp.sum(jnp.where(is_row, blk, 0.0), axis=0, keepdims=True)


def _set_row(ref, bidx, cols, row):
    idx, blk, is_row = _row_group(ref, bidx, cols)
    ref[idx] = jnp.where(is_row, row, blk)


BF16_ROWS = 16


def _one_hot_rows(row, bidx):
    sub = lax.broadcasted_iota(jnp.int32, (DEC_BATCH, row.shape[1]), 0)
    return jnp.where(sub == bidx, row, 0.0).astype(BF16)


def _row_times(row, mat_nt=None, mat=None):
    lhs = jnp.broadcast_to(row.astype(BF16), (BF16_ROWS, row.shape[1]))
    out = _dot(lhs, mat) if mat is not None else _dot_nt(lhs, mat_nt)
    return out[0:1]


def _hgrn_step_kernel(q_ref, f_ref, v_ref, og_ref, lb_ref, nw_ref, s_ref, o_ref, so_ref,
                      dec_t, k_t, q_s, *, first, zero_rest):
    i = pl.program_id(0)

    @pl.when(i == 0)
    def _():
        g, k = _hgrn_gates(f_ref[...], lb_ref[0], first)
        dec = jnp.exp(g)
        q_s[...] = _silu(q_ref[...]) * (HG_DK ** -0.5)
        o_ref[...] = jnp.zeros_like(o_ref)
        for h in range(HG_HEADS):
            hs = slice(h * HG_DK, (h + 1) * HG_DK)
            dec_t[hs, :] = dec[:, hs].T
            k_t[hs, :] = k[:, hs].T.astype(BF16)

    if zero_rest:
        so_ref[1:] = jnp.zeros((DEPTH - 1,) + so_ref.shape[1:], F32)

    lane = lax.broadcasted_iota(jnp.int32, (HG_DK, DEC_BATCH), 1)

    def body(j, carry):
        bidx = i * HG_STEP_NB + j
        m = lane == bidx
        for h in range(HG_HEADS):
            hs = slice(h * HG_DK, (h + 1) * HG_DK)
            d_col = _column(dec_t[hs, :], m)
            kv = _dot(k_t[hs, :], _one_hot_rows(_get_row(v_ref, bidx, hs), bidx))
            s_new = d_col * s_ref[0, j, h] + kv
            so_ref[0, j, h] = s_new
            o_row = _row_times(_get_row(q_s, bidx, hs), mat=s_new.astype(BF16))
            _set_row(o_ref, bidx, hs, _rms(o_row, nw_ref[0]) * _silu(_get_row(og_ref, bidx, hs)))
        return carry

    lax.fori_loop(0, HG_STEP_NB, body, 0)


def _stacked_state_specs(state, nb, l, out_state):
    tail = state.shape[2:]
    zeros = (0,) * len(tail)
    in_spec = pl.BlockSpec((1, nb) + tail, lambda i: (l, i) + zeros)
    if out_state is None:
        return in_spec, pl.BlockSpec((DEPTH, nb) + tail, lambda i: (0, i) + zeros), [], [], {}
    out_spec = pl.BlockSpec((1, nb) + tail, lambda i: (l, i) + zeros)
    return in_spec, out_spec, [pl.BlockSpec(memory_space=pl.ANY)], [out_state], None


def _hgrn_step(proj, state, lbs, hg_norm, out_state, l):
    nb = HG_STEP_NB
    blk = lambda c0: pl.BlockSpec((DEC_BATCH, D_MODEL), lambda i: (0, c0 // D_MODEL))
    sspec, ospec, extra_specs, extra_args, aliases = _stacked_state_specs(state, nb, l, out_state)
    in_specs = [
        blk(COL_Q), blk(COL_F), blk(COL_I), blk(COL_OG),
        pl.BlockSpec((1, 1, D_MODEL), lambda i: (l, 0, 0)),
        pl.BlockSpec((1, 1, HG_DV), lambda i: (l, 0, 0)),
        sspec,
    ] + extra_specs
    args = [proj, proj, proj, proj, lbs, hg_norm, state] + extra_args
    if aliases is None:
        aliases = {len(args) - 1: 1}

    def body(*refs):
        ins, rest = refs[:7], refs[len(args):]
        _hgrn_step_kernel(*ins, *rest, first=(l == 0), zero_rest=(out_state is None))

    return pl.pallas_call(
        body,
        grid=(DEC_BATCH // nb,),
        in_specs=in_specs,
        out_specs=[pl.BlockSpec((DEC_BATCH, D_MODEL), lambda i: (0, 0)), ospec],
        out_shape=[
            jax.ShapeDtypeStruct((DEC_BATCH, D_MODEL), F32),
            jax.ShapeDtypeStruct(state.shape, F32),
        ],
        scratch_shapes=[pltpu.VMEM((D_MODEL, DEC_BATCH), F32), pltpu.VMEM((D_MODEL, DEC_BATCH), BF16),
                        pltpu.VMEM((DEC_BATCH, D_MODEL), F32)],
        input_output_aliases=aliases,
        compiler_params=_cparams("arbitrary"),
        name="hgrn_step",
    )(*args)


def _ssm_prep_kernel(xs_ref, bm_ref, cm_ref, dt_ref, cst_ref, w_ref, cb_ref,
                     alog_ref, exp_ref,
                     xc_ref, bc_ref, cc_ref, dec_ref, xdtt_ref, cnew_ref):
    ch = SSM_CONV_CH
    parts = ((xs_ref, 0, SSM_INNER, xc_ref),
             (bm_ref, SSM_INNER, SSM_INNER + 1024, bc_ref),
             (cm_ref, SSM_INNER + 1024, ch, cc_ref))
    for u_ref, c0, c1, out_ref in parts:
        u = u_ref[...]
        acc = cb_ref[:, c0:c1] + w_ref[SSM_CONV - 1:SSM_CONV, c0:c1] * u
        for j in range(SSM_CONV - 1):
            acc = acc + w_ref[j:j + 1, c0:c1] * cst_ref[:, j * ch + c0:j * ch + c1]
        out_ref[...] = _silu(acc)
        cnew_ref[:, (SSM_CONV - 2) * ch + c0:(SSM_CONV - 2) * ch + c1] = u
    cnew_ref[:, 0:(SSM_CONV - 2) * ch] = cst_ref[:, ch:(SSM_CONV - 1) * ch]
    dtv = _dot_sel(dt_ref[...], exp_ref[...])
    dec_ref[...] = jnp.exp(dtv * -jnp.exp(alog_ref[...]))
    xdt = xc_ref[...] * dtv
    for m in range(SSM_INNER // LANE):
        ms = slice(m * LANE, (m + 1) * LANE)
        xdtt_ref[ms, :] = xdt[:, ms].T.astype(BF16)


def _ssm_prep(proj, dt, conv_state, p, l):
    n = DEC_BATCH
    full = lambda shape: pl.BlockSpec(shape, lambda i: tuple(0 for _ in shape))
    lspec = lambda rows, width: pl.BlockSpec((1, rows, width), lambda i: (l, 0, 0))

    def body(xs, bm, cm, dtr, cst, w, cb, alog, ex, *outs):
        _ssm_prep_kernel(xs, bm, cm, dtr, cst.at[0], w.at[0], cb.at[0], alog.at[0], ex, *outs)

    return pl.pallas_call(
        body,
        grid=(1,),
        in_specs=[
            pl.BlockSpec((n, SSM_INNER), lambda i: (0, COL_XBC // SSM_INNER)),
            pl.BlockSpec((n, 1024), lambda i: (0, (COL_XBC + SSM_INNER) // 1024)),
            pl.BlockSpec((n, 1024), lambda i: (0, (COL_XBC + SSM_INNER) // 1024 + 1)),
            full((n, DT_PAD)),
            pl.BlockSpec((1, n, (SSM_CONV - 1) * SSM_CONV_CH), lambda i: (l, 0, 0)),
            lspec(SSM_CONV, SSM_CONV_CH), lspec(1, SSM_CONV_CH),
            lspec(1, SSM_INNER),
            full((DT_PAD, SSM_INNER)),
        ],
        out_specs=[
            full((n, SSM_INNER)), full((n, 1024)), full((n, 1024)),
            full((n, SSM_INNER)), full((SSM_INNER, n)),
            full((n, (SSM_CONV - 1) * SSM_CONV_CH)),
        ],
        out_shape=[
            jax.ShapeDtypeStruct((n, SSM_INNER), F32),
            jax.ShapeDtypeStruct((n, 1024), F32),
            jax.ShapeDtypeStruct((n, 1024), F32),
            jax.ShapeDtypeStruct((n, SSM_INNER), F32),
            jax.ShapeDtypeStruct((SSM_INNER, n), BF16),
            jax.ShapeDtypeStruct((n, (SSM_CONV - 1) * SSM_CONV_CH), F32),
        ],
        compiler_params=_cparams("arbitrary"),
        name="ssm_prep",
    )(proj, proj, proj, dt, conv_state, p['conv_w'], p['conv_b'], p['a_log_x'], p['expand_all'])


SSM_STEP_NB = 4


def _ssm_step_kernel(dec_ref, xdtt_ref, bc_ref, cc_ref, s_ref, y_ref, so_ref, *, zero_rest):
    i = pl.program_id(0)

    @pl.when(i == 0)
    def _():
        y_ref[...] = jnp.zeros_like(y_ref)

    if zero_rest:
        so_ref[1:] = jnp.zeros((DEPTH - 1,) + so_ref.shape[1:], F32)

    def body(j, carry):
        bidx = i * SSM_STEP_NB + j
        for g in range(SSM_GROUPS):
            rs = slice(g * GROUP_W, (g + 1) * GROUP_W)
            hs = slice(g * SSM_HPG, (g + 1) * SSM_HPG)
            ns = slice(g * SSM_STATE, (g + 1) * SSM_STATE)
            dec_row = _get_row(dec_ref, bidx, rs)
            decay = jnp.concatenate(
                [jnp.broadcast_to(dec_row[:, k * SSM_HEADDIM:k * SSM_HEADDIM + 1],
                                  (SSM_HEADDIM, SSM_STATE)) for k in range(SSM_HPG)], axis=0)
            inc = _dot(xdtt_ref[rs, :], _one_hot_rows(_get_row(bc_ref, bidx, ns), bidx))
            h = s_ref[0, j, hs].reshape(GROUP_W, SSM_STATE)
            h_new = decay * h + inc
            so_ref[0, j, hs] = h_new.reshape(SSM_HPG, SSM_HEADDIM, SSM_STATE)
            y_row = _row_times(_get_row(cc_ref, bidx, ns), mat_nt=h_new.astype(BF16))
            _set_row(y_ref, bidx, rs, y_row)
        return carry

    lax.fori_loop(0, SSM_STEP_NB, body, 0)


def _ssm_step(dec, xdtt, bc, cc, state, out_state, l):
    nb = SSM_STEP_NB
    full = lambda shape: pl.BlockSpec(shape, lambda i: tuple(0 for _ in shape))
    sspec, ospec, extra_specs, extra_args, aliases = _stacked_state_specs(state, nb, l, out_state)
    in_specs = [full(dec.shape), full(xdtt.shape), full(bc.shape), full(cc.shape),
                sspec] + extra_specs
    args = [dec, xdtt, bc, cc, state] + extra_args
    if aliases is None:
        aliases = {len(args) - 1: 1}

    def body(*refs):
        ins, rest = refs[:5], refs[len(args):]
        _ssm_step_kernel(*ins, *rest, zero_rest=(out_state is None))

    return pl.pallas_call(
        body,
        grid=(DEC_BATCH // nb,),
        in_specs=in_specs,
        out_specs=[full((DEC_BATCH, SSM_INNER)), ospec],
        out_shape=[
            jax.ShapeDtypeStruct((DEC_BATCH, SSM_INNER), F32),
            jax.ShapeDtypeStruct(state.shape, F32),
        ],
        input_output_aliases=aliases,
        compiler_params=_cparams("arbitrary"),
        name="ssm_step",
    )(*args)


def _ssm_post_kernel(y_in_ref, xc_ref, z_ref, dsk_ref, nw_ref, y_ref):
    for g in range(SSM_GROUPS):
        gs = slice(g * GROUP_W, (g + 1) * GROUP_W)
        y = (y_in_ref[:, gs] + dsk_ref[:, gs] * xc_ref[:, gs]) * _silu(z_ref[:, gs])
        y_ref[:, gs] = _rms(y, nw_ref[:, gs])


def _ssm_post(y, xc, proj, p, l):
    n = DEC_BATCH
    full = lambda shape: pl.BlockSpec(shape, lambda i: tuple(0 for _ in shape))
    lspec = pl.BlockSpec((1, 1, SSM_INNER), lambda i: (l, 0, 0))

    def body(y_in, xc_r, z_r, dsk, nw, y_r):
        _ssm_post_kernel(y_in, xc_r, z_r, dsk.at[0], nw.at[0], y_r)

    return pl.pallas_call(
        body,
        grid=(1,),
        in_specs=[full((n, SSM_INNER)), full((n, SSM_INNER)),
                  pl.BlockSpec((n, SSM_INNER), lambda i: (0, COL_Z // SSM_INNER)),
                  lspec, lspec],
        out_specs=full((n, SSM_INNER)),
        out_shape=jax.ShapeDtypeStruct((n, SSM_INNER), F32),
        compiler_params=_cparams("arbitrary"),
        name="ssm_post",
    )(y, xc, proj, p['d_skip_x'], p['ssm_norm'])


def _merge_kernel(x_ref, oh_ref, y_ref, ga_ref, gb_ref, g1_ref, bma_ref, bmb_ref,
                  wa_ref, wb_ref, wo_ref, o_ref):
    a = _dot(oh_ref[...].astype(BF16), wa_ref[0])
    b = _dot(y_ref[...].astype(BF16), wb_ref[0])
    u = _sigmoid(ga_ref[...] + bma_ref[0]) * a + _sigmoid(gb_ref[...] + bmb_ref[0]) * b
    o_ref[...] = x_ref[...] + g1_ref[0] * _dot(u.astype(BF16), wo_ref[0])


def _merge(x, o_hg, y, proj, mod, p, l, tm, rows_per_group):
    rows = x.shape[0]
    wspec = lambda k: pl.BlockSpec((1, k, D_MODEL), lambda i: (l, 0, 0))
    return pl.pallas_call(
        _merge_kernel,
        grid=(rows // tm,),
        in_specs=[
            pl.BlockSpec((tm, D_MODEL), lambda i: (i, 0)),
            pl.BlockSpec((tm, D_MODEL), lambda i: (i, 0)),
            pl.BlockSpec((tm, SSM_INNER), lambda i: (i, 0)),
            pl.BlockSpec((tm, D_MODEL), lambda i: (i, COL_GA // D_MODEL)),
            pl.BlockSpec((tm, D_MODEL), lambda i: (i, COL_GB // D_MODEL)),
            _mod_spec(mod, tm, rows_per_group, ADA_G1),
            pl.BlockSpec((1, 1, D_MODEL), lambda i: (l, 0, 0)),
            pl.BlockSpec((1, 1, D_MODEL), lambda i: (l, 0, 1)),
            wspec(D_MODEL), wspec(SSM_INNER), wspec(D_MODEL),
        ],
        out_specs=pl.BlockSpec((tm, D_MODEL), lambda i: (i, 0)),
        out_shape=jax.ShapeDtypeStruct((rows, D_MODEL), F32),
        compiler_params=_cparams("arbitrary"),
        name="merge",
    )(x, o_hg, y, proj, proj, mod, p['b_merge'], p['b_merge'], p['w_br_a'], p['w_br_b'],
      p['w_out'])


FF_CHUNK = 1024


def _mlp_kernel(x_ref, sc_ref, sh_ref, g_ref, nw_ref, wu_ref, wd_ref, fw_ref, o_ref, *, final):
    x = x_ref[...]
    h = (_rms(x, nw_ref[0]) * (1.0 + sc_ref[0]) + sh_ref[0]).astype(BF16)
    acc = jnp.zeros(x.shape, F32)
    for c in range(D_FF // FF_CHUNK):
        cs = slice(c * FF_CHUNK, (c + 1) * FF_CHUNK)
        a = jnp.maximum(_dot(h, wu_ref[0, :, cs]), 0.0)
        acc = acc + _dot((a * a).astype(BF16), wd_ref[0, cs, :])
    out = x + g_ref[0] * acc
    if final:
        out = _rms(out, fw_ref[...])
    o_ref[...] = out


def _mlp(x, mod, p, norm_final, l, tm, rows_per_group):
    rows = x.shape[0]
    return pl.pallas_call(
        functools.partial(_mlp_kernel, final=(l == DEPTH - 1)),
        grid=(rows // tm,),
        in_specs=[
            pl.BlockSpec((tm, D_MODEL), lambda i: (i, 0)),
            _mod_spec(mod, tm, rows_per_group, ADA_SC2),
            _mod_spec(mod, tm, rows_per_group, ADA_SH2),
            _mod_spec(mod, tm, rows_per_group, ADA_G2),
            pl.BlockSpec((1, 1, D_MODEL), lambda i: (l, 0, 0)),
            pl.BlockSpec((1, D_MODEL, D_FF), lambda i: (l, 0, 0)),
            pl.BlockSpec((1, D_FF, D_MODEL), lambda i: (l, 0, 0)),
            pl.BlockSpec((1, D_MODEL), lambda i: (0, 0)),
        ],
        out_specs=pl.BlockSpec((tm, D_MODEL), lambda i: (i, 0)),
        out_shape=jax.ShapeDtypeStruct((rows, D_MODEL), F32),
        compiler_params=_cparams("arbitrary"),
        name="mlp",
    )(x, mod, mod, mod, p['norm_mlp'], p['w_up'], p['w_down'], norm_final)


def _prepare_params(w_ada, b_ada, norm_mix, w_in, b_merge, hg_norm, conv_w, conv_b, dt_bias,
                    a_log, d_skip, ssm_norm, w_br_a, w_br_b, w_out, norm_mlp, w_up, w_down):
    dt0 = COL_XBC + SSM_CONV_CH
    w_main = jnp.concatenate([w_in[:, :, :dt0], w_in[:, :, dt0 + SSM_HEADS:]], axis=-1)
    w_dt = jnp.pad(w_in[:, :, dt0:dt0 + SSM_HEADS], ((0, 0), (0, 0), (0, DT_PAD - SSM_HEADS)))
    per_head = lambda a: jnp.repeat(a, SSM_HEADDIM, axis=-1).reshape(DEPTH, 1, SSM_INNER)
    narrow = lambda a: jnp.pad(a, ((0, 0), (0, DT_PAD - SSM_HEADS))).reshape(DEPTH, 1, DT_PAD)
    head_of_col = jnp.arange(SSM_INNER) // SSM_HEADDIM
    expand_all = (jnp.arange(DT_PAD)[:, None] == head_of_col[None, :]).astype(BF16)
    expand_g = expand_all.reshape(DT_PAD, SSM_GROUPS, GROUP_W).transpose(1, 0, 2)
    return {
        'norm_mix': norm_mix.reshape(DEPTH, 1, D_MODEL),
        'w_main': w_main.astype(BF16), 'w_dt': w_dt.astype(BF16),
        'b_merge': b_merge.reshape(DEPTH, 1, 2 * D_MODEL),
        'hg_norm': hg_norm.reshape(DEPTH, 1, HG_DV),
        'conv_w': conv_w, 'conv_b': conv_b.reshape(DEPTH, 1, SSM_CONV_CH),
        'dt_bias_n': narrow(dt_bias), 'a_log_n': narrow(a_log),
        'a_log_x': per_head(a_log), 'd_skip_x': per_head(d_skip),
        'ssm_norm': ssm_norm.reshape(DEPTH, 1, SSM_INNER),
        'expand_all': expand_all, 'expand_g': expand_g,
        'w_br_a': w_br_a.astype(BF16), 'w_br_b': w_br_b.astype(BF16),
        'w_out': w_out.astype(BF16),
        'norm_mlp': norm_mlp.reshape(DEPTH, 1, D_MODEL),
        'w_up': w_up.astype(BF16), 'w_down': w_down.astype(BF16),
    }


def kernel(x_prompt, x_sample, state_hgrn, state_ssm, state_conv, c_prompt, c_sample, w_ada, b_ada, norm_mix, w_in, b_merge, lower_bounds, hg_norm, conv_w, conv_b, dt_bias, a_log, d_skip, ssm_norm, w_br_a, w_br_b, w_out, norm_mlp, w_up, w_down, norm_final):
    p = _prepare_params(w_ada, b_ada, norm_mix, w_in, b_merge, hg_norm, conv_w, conv_b, dt_bias,
                        a_log, d_skip, ssm_norm, w_br_a, w_br_b, w_out, norm_mlp, w_up, w_down)
    lbs = _lower_bounds(lower_bounds).reshape(DEPTH, 1, HG_HEADS * HG_DK)
    mod = _modulation(jnp.concatenate([c_prompt, c_sample], axis=0), w_ada, b_ada)
    mod_p = mod[:, :BATCH].reshape(DEPTH, BATCH, 1, N_ADA * D_MODEL)
    mod_s = mod[:, BATCH:].reshape(DEPTH, 1, DEC_BATCH, N_ADA * D_MODEL)
    norm_final2 = norm_final.reshape(1, D_MODEL)
    conv_state = state_conv.reshape(DEPTH, DEC_BATCH, (SSM_CONV - 1) * SSM_CONV_CH)

    xp = x_prompt.reshape(BATCH * SEQ, D_MODEL)
    xs = x_sample.reshape(DEC_BATCH, D_MODEL)
    hg_p, ssm_p, cv_p, cv_s = [], [], [], []
    hg_s = ssm_s = None
    for l in range(DEPTH):
        proj, dt, cum = _inproj(xp, mod_p[l], p, l, SEQ, SEQ, 1)
        o_hg, s_new, y, h_new = _mixer_prompt(proj, dt, cum, lbs, p, l)
        hg_p.append(s_new)
        ssm_p.append(h_new)
        cv_p.append(proj.reshape(BATCH, SEQ, PROJ_W)[:, SEQ - (SSM_CONV - 1):,
                                                       COL_XBC:COL_XBC + SSM_CONV_CH])
        xp = _merge(xp, o_hg, y, proj, mod_p[l], p, l, 512, SEQ)
        xp = _mlp(xp, mod_p[l], p, norm_final2, l, 512, SEQ)
        proj, dt, _ = _inproj(xs, mod_s[l], p, l, DEC_BATCH, DEC_BATCH, 3)
        o_hg, hg_s = _hgrn_step(proj, state_hgrn, lbs, p['hg_norm'], hg_s, l)
        xc, bc, cc, dect, xdtt, cnew = _ssm_prep(proj, dt, conv_state, p, l)
        yt, ssm_s = _ssm_step(dect, xdtt, bc, cc, state_ssm, ssm_s, l)
        y = _ssm_post(yt, xc, proj, p, l)
        cv_s.append(cnew.reshape(DEC_BATCH, SSM_CONV - 1, SSM_CONV_CH))
        xs = _merge(xs, o_hg, y, proj, mod_s[l], p, l, DEC_BATCH, DEC_BATCH)
        xs = _mlp(xs, mod_s[l], p, norm_final2, l, DEC_BATCH, DEC_BATCH)
    return (xp.reshape(BATCH, SEQ, D_MODEL), xs.reshape(DEC_BATCH, 1, D_MODEL),
            jnp.stack(hg_p), jnp.stack(ssm_p), jnp.stack(cv_p),
            hg_s, ssm_s, jnp.stack(cv_s))
```

```python
import functools

import jax
import jax.numpy as jnp
import numpy as np
from jax import lax
from jax.experimental import pallas as pl
from jax.experimental.pallas import tpu as pltpu

F32 = jnp.float32
BF16 = jnp.bfloat16

D_MODEL = 1024
BATCH = 8
SEQ = 2048
DEPTH = 4
DEC_BATCH = 128
HG_HEADS = 8
HG_DK = 128
HG_DV = 128
SSM_INNER = 2048
SSM_HEADDIM = 64
SSM_HEADS = 32
SSM_GROUPS = 8
SSM_HPG = 4
SSM_STATE = 128
SSM_CONV = 4
SSM_CONV_CH = 4096
GROUP_W = SSM_HPG * SSM_HEADDIM
D_FF = 4096
N_ADA = 6
EPS = 1e-6
DT_PAD = 128
COL_Q, COL_F, COL_I, COL_OG = 0, 1024, 2048, 3072
COL_Z, COL_XBC, COL_GA, COL_GB = 4096, 6144, 10240, 11264
PROJ_W = 12288
ADA_SH1, ADA_SC1, ADA_G1, ADA_SH2, ADA_SC2, ADA_G2 = range(6)

LANE = 128
SUBLANES = 8
TILE = 128
VMEM_LIMIT = 56 * 1024 * 1024


def _cparams(*sem):
    return pltpu.CompilerParams(dimension_semantics=sem, vmem_limit_bytes=VMEM_LIMIT)


def _sigmoid(x):
    return 0.5 * jnp.tanh(0.5 * x) + 0.5


def _silu(x):
    return x * _sigmoid(x)


def _softplus(x):
    return jnp.maximum(x, 0.0) + jnp.log1p(jnp.exp(-jnp.abs(x)))


def _dot(a, b):
    return jnp.dot(a, b, preferred_element_type=F32)


def _dot_nt(a, b):
    return lax.dot_general(a, b, (((1,), (1,)), ((), ())), preferred_element_type=F32)


def _split(a, terms):
    parts = []
    for _ in range(terms - 1):
        hi = a.astype(BF16)
        parts.append(hi)
        a = a - hi.astype(F32)
    parts.append(a.astype(BF16))
    return parts


def _sel_dot(sel, a, terms=3):
    return _dot(jnp.concatenate([sel] * terms, axis=1), jnp.concatenate(_split(a, terms), axis=0))


def _dot_sel(a, sel, terms=3):
    return _dot(jnp.concatenate(_split(a, terms), axis=1), jnp.concatenate([sel] * terms, axis=0))


def _rms(x, w):
    ms = jnp.mean(x * x, axis=-1, keepdims=True)
    return x * lax.rsqrt(ms + EPS) * w


def _hgrn_gates(f, lb, first):
    if first:
        g = jnp.minimum(f, 0.0) - jnp.log1p(jnp.exp(-jnp.abs(f)))
        k = _sigmoid(-f)
    else:
        fg = lb + (1.0 - lb) * _sigmoid(f)
        g = jnp.log(fg)
        k = 1.0 - fg
    return g, k


def _lbs_kernel(lb_ref, o_ref):
    x = lb_ref[...]
    m = jnp.max(x, axis=0, keepdims=True)
    e = jnp.exp(x - m)
    p = e / jnp.sum(e, axis=0, keepdims=True)
    acc = jnp.zeros_like(p[0:1])
    rows = [acc]
    for l in range(1, DEPTH):
        acc = acc + p[l:l + 1]
        rows.append(acc)
    o_ref[...] = jnp.concatenate(rows, axis=0)


def _lower_bounds(lower_bounds):
    return pl.pallas_call(
        _lbs_kernel, out_shape=jax.ShapeDtypeStruct(lower_bounds.shape, F32), name="lbs",
    )(lower_bounds)


MOD_STREAMS = 3


def _mod_kernel(c_ref, *refs):
    w_refs, b_ref, o_ref = refs[:MOD_STREAMS], refs[MOD_STREAMS], refs[MOD_STREAMS + 1]
    a = _silu(c_ref[...]).astype(BF16)
    for k, w_ref in enumerate(w_refs):
        cs = slice(k * D_MODEL, (k + 1) * D_MODEL)
        o_ref[0, :, cs] = _dot(a, w_ref[0].astype(BF16)) + b_ref[0, :, cs]


def _modulation(c_all, w_ada, b_ada):
    n = c_all.shape[0]
    tn = MOD_STREAMS * D_MODEL
    w_spec = lambda k: pl.BlockSpec((1, D_MODEL, D_MODEL), lambda l, j: (l, 0, j * MOD_STREAMS + k))
    return pl.pallas_call(
        _mod_kernel,
        grid=(DEPTH, N_ADA * D_MODEL // tn),
        in_specs=[pl.BlockSpec((n, D_MODEL), lambda l, j: (0, 0))]
        + [w_spec(k) for k in range(MOD_STREAMS)]
        + [pl.BlockSpec((1, 1, tn), lambda l, j: (l, 0, j))],
        out_specs=pl.BlockSpec((1, n, tn), lambda l, j: (l, 0, j)),
        out_shape=jax.ShapeDtypeStruct((DEPTH, n, N_ADA * D_MODEL), F32),
        compiler_params=_cparams("arbitrary", "arbitrary"),
        name="modulation",
    )(c_all, *([w_ada] * MOD_STREAMS), b_ada.reshape(DEPTH, 1, N_ADA * D_MODEL))


def _mod_spec(mod, tm, rows_per_group, piece):
    if mod.shape[1] == 1:
        tiles = rows_per_group // tm
        return pl.BlockSpec((1, 1, D_MODEL), lambda i, *_: (i // tiles, 0, piece))
    return pl.BlockSpec((1, tm, D_MODEL), lambda i, *_: (0, i, piece))


def _inproj_kernel(x_ref, sc_ref, sh_ref, nw_ref, *refs, streams):
    w_refs = refs[:streams]
    wdt_ref, dtb_ref, alog_ref, o_ref, odt_ref, ocum_ref, h_scr = refs[streams:]

    @pl.when(pl.program_id(1) == 0)
    def _():
        h = _rms(x_ref[...], nw_ref[0]) * (1.0 + sc_ref[0]) + sh_ref[0]
        hb = h.astype(BF16)
        h_scr[...] = hb
        dtv = _softplus(_dot(hb, wdt_ref[0]) + dtb_ref[0])
        odt_ref[...] = dtv
        la = dtv * -jnp.exp(alog_ref[0])
        row = lax.broadcasted_iota(jnp.int32, (TILE, TILE), 0)
        col = lax.broadcasted_iota(jnp.int32, (TILE, TILE), 1)
        tri = jnp.where(row >= col, 1.0, 0.0).astype(BF16)
        for i in range(la.shape[0] // TILE):
            rs = slice(i * TILE, (i + 1) * TILE)
            ocum_ref[rs, :] = _sel_dot(tri, la[rs] * LOG2E)

    h = h_scr[...]
    for k, w_ref in enumerate(w_refs):
        o_ref[:, k * INPROJ_TN:(k + 1) * INPROJ_TN] = _dot(h, w_ref[0])


INPROJ_TN = 1024


def _inproj(x, mod, p, l, tm, rows_per_group, streams):
    rows = x.shape[0]
    tn = streams * INPROJ_TN
    narrow = pl.BlockSpec((1, 1, DT_PAD), lambda i, j: (l, 0, 0))
    w_spec = lambda k: pl.BlockSpec((1, D_MODEL, INPROJ_TN), lambda i, j: (l, 0, j * streams + k))
    return pl.pallas_call(
        functools.partial(_inproj_kernel, streams=streams),
        grid=(rows // tm, PROJ_W // tn),
        in_specs=[
            pl.BlockSpec((tm, D_MODEL), lambda i, j: (i, 0)),
            _mod_spec(mod, tm, rows_per_group, ADA_SC1),
            _mod_spec(mod, tm, rows_per_group, ADA_SH1),
            pl.BlockSpec((1, 1, D_MODEL), lambda i, j: (l, 0, 0)),
        ] + [w_spec(k) for k in range(streams)] + [
            pl.BlockSpec((1, D_MODEL, DT_PAD), lambda i, j: (l, 0, 0)),
            narrow, narrow,
        ],
        out_specs=[
            pl.BlockSpec((tm, tn), lambda i, j: (i, j)),
            pl.BlockSpec((tm, DT_PAD), lambda i, j: (i, 0)),
            pl.BlockSpec((tm, DT_PAD), lambda i, j: (i, 0)),
        ],
        out_shape=[
            jax.ShapeDtypeStruct((rows, PROJ_W), F32),
            jax.ShapeDtypeStruct((rows, DT_PAD), F32),
            jax.ShapeDtypeStruct((rows, DT_PAD), F32),
        ],
        scratch_shapes=[pltpu.VMEM((tm, D_MODEL), BF16)],
        compiler_params=_cparams("arbitrary", "arbitrary"),
        name="inproj",
    )(x, mod, mod, p['norm_mix'], *([p['w_main']] * streams), p['w_dt'], p['dt_bias_n'],
      p['a_log_n'])


HG_LEVELS = 7
HG_HB = 4
HG_SPLIT = 2
LOG2E = 1.4426950408889634


def _hgrn_sum_ranges():
    r = np.arange(TILE)[:, None]
    c = np.arange(TILE)[None, :]
    mats = [c <= r, c > r]
    for s in range(HG_LEVELS):
        h = 1 << s
        right = ((r >> s) & 1) == 1
        in_right = (c >= (r & ~(h - 1))) & (c <= r)
        in_left = (c > r) & (c <= (r | (h - 1)))
        mats.append(np.where(right, in_right, in_left))
    m = np.concatenate(mats, axis=0).astype(np.float32)
    return np.tile(m, (1, HG_SPLIT))


def _hgrn_level_masks():
    r = np.arange(TILE)[:, None]
    c = np.arange(TILE)[None, :]
    x = r ^ c
    top = np.where(x > 0, np.floor(np.log2(np.maximum(x, 1))).astype(np.int64), -1)
    masks = [r == c] + [(r > c) & (top == s) for s in range(HG_LEVELS)]
    return np.stack(masks).astype(np.float32)


def _hgrn_tile(rows, q_ref, f_ref, v_ref, og_ref, lb, nw, sums_ref, mask_ref, o_ref, st_scr, first):
    g, k = _hgrn_gates(f_ref[rows, :], lb, first)
    g2 = jnp.concatenate(_split(g * LOG2E, HG_SPLIT), axis=0)
    decay = jnp.exp2(_dot(sums_ref[...], g2))
    q = _silu(q_ref[rows, :]) * (HG_DK ** -0.5)
    v = v_ref[rows, :]
    og = _silu(og_ref[rows, :])
    q_in = (q * decay[0:TILE]).astype(BF16)
    k_out = (k * decay[TILE:2 * TILE]).astype(BF16)
    end = decay[TILE - 1:TILE]
    qb = q.astype(BF16)
    kb = k.astype(BF16)
    vb = v.astype(BF16)
    outs = []
    for hh in range(HG_HB):
        cs = slice(hh * HG_DK, (hh + 1) * HG_DK)
        st = st_scr[hh]
        p = mask_ref[0] * _dot_nt(qb[:, cs], kb[:, cs])
        for s in range(HG_LEVELS):
            a = decay[(2 + s) * TILE:(3 + s) * TILE, cs].astype(BF16)
            p = p + mask_ref[1 + s] * _dot_nt(qb[:, cs] * a, kb[:, cs] * a)
        o = _dot_nt(q_in[:, cs], st.astype(BF16)) + _dot(p.astype(BF16), vb[:, cs])
        st_scr[hh] = st * end[:, cs] + _dot(v[:, cs].T.astype(BF16), k_out[:, cs])
        outs.append(_rms(o, nw) * og[:, cs])
    o_ref[rows, :] = jnp.concatenate(outs, axis=1).astype(o_ref.dtype)


def _conv_silu_tile(ref, cols, w_ref, b_ref, tail, r0, t):
    cur = ref[pl.ds(r0, TILE), cols]
    prev = ref[pl.ds(pl.multiple_of(jnp.maximum(r0 - SUBLANES, 0), SUBLANES), SUBLANES), cols]
    prev = jnp.where(t > 0, prev, tail)
    ext = jnp.concatenate([prev, cur], axis=0)
    acc = b_ref[:, cols] + w_ref[SSM_CONV - 1:SSM_CONV, cols] * cur
    for j in range(SSM_CONV - 1):
        acc = acc + w_ref[j:j + 1, cols] * ext[5 + j:5 + j + TILE, :]
    return _silu(acc)


def _ssd_tile(rows, r0, t, gi, x_ref, bm_ref, cm_ref, z_ref, dt_ref, cum_ref,
              wx_ref, wb_ref, wc_ref, cbx_ref, cbb_ref, cbc_ref, dsk_ref, nw_ref, exp_ref,
              tail_scr, y_ref, ht_scr, causal, head_of_lane):
    xs = slice(gi * GROUP_W, (gi + 1) * GROUP_W)
    ns = slice(gi * SSM_STATE, (gi + 1) * SSM_STATE)
    b_off = MIX_GROUPS * GROUP_W
    c_off = b_off + MIX_GROUPS * SSM_STATE
    tail_b = slice(b_off + gi * SSM_STATE, b_off + (gi + 1) * SSM_STATE)
    tail_c = slice(c_off + gi * SSM_STATE, c_off + (gi + 1) * SSM_STATE)
    xc = _conv_silu_tile(x_ref, xs, wx_ref, cbx_ref, tail_scr[:, xs], r0, t)
    bc = _conv_silu_tile(bm_ref, ns, wb_ref, cbb_ref, tail_scr[:, tail_b], r0, t)
    cc = _conv_silu_tile(cm_ref, ns, wc_ref, cbc_ref, tail_scr[:, tail_c], r0, t)
    expand = exp_ref[gi]
    dtv = _dot_sel(dt_ref[rows, :], expand, terms=2)
    cum = _dot_sel(cum_ref[rows, :], expand)
    cum_end = cum[TILE - 1:TILE, :]
    cum_t = cum.T
    xdt = xc * dtv
    xdtb = xdt.astype(BF16)
    ccb = cc.astype(BF16)
    cb = jnp.where(causal, _dot_nt(ccb, bc.astype(BF16)), 0.0)
    ht = ht_scr[gi]
    weights, values = [], []
    for k in range(SSM_HPG):
        c_col = cum[:, k * SSM_HEADDIM:k * SSM_HEADDIM + 1]
        c_row = cum_t[k * SSM_HEADDIM:k * SSM_HEADDIM + 1, :]
        decay = jnp.exp2(jnp.minimum(c_col - c_row, 0.0))
        weights.append((cb * decay).astype(BF16))
        values.append(jnp.where(head_of_lane == k, xdtb, jnp.zeros_like(xdtb)))
    y = (_dot(ccb, ht.astype(BF16)) * jnp.exp2(cum)
         + _dot(jnp.concatenate(weights, axis=1), jnp.concatenate(values, axis=0)))
    x_out = (xdt * jnp.exp2(cum_end - cum)).astype(BF16)
    ht_scr[gi] = ht * jnp.exp2(cum_end) + _dot(bc.T.astype(BF16), x_out)
    y = (y + dsk_ref[:, xs] * xc) * _silu(z_ref[rows, xs])
    y_ref[rows, xs] = _rms(y, nw_ref[:, xs]).astype(y_ref.dtype)


MIX_ROWS = 1024
MIX_GROUPS = SSM_GROUPS // (HG_HEADS // HG_HB)


def _mixer_prompt_kernel(q_ref, f_ref, v_ref, og_ref, lb_ref, hnw_ref, sums_ref, mask_ref,
                         x_ref, bm_ref, cm_ref, z_ref, dt_ref, cum_ref,
                         wx_ref, wb_ref, wc_ref, cbx_ref, cbb_ref, cbc_ref,
                         dsk_ref, snw_ref, exp_ref,
                         o_ref, s_ref, y_ref, h_ref, st_scr, ht_scr, tail_scr, *, first):
    chunk = pl.program_id(2)

    @pl.when(chunk == 0)
    def _():
        st_scr[...] = jnp.zeros_like(st_scr)
        ht_scr[...] = jnp.zeros_like(ht_scr)
        tail_scr[...] = jnp.zeros_like(tail_scr)

    row = lax.broadcasted_iota(jnp.int32, (TILE, TILE), 0)
    col = lax.broadcasted_iota(jnp.int32, (TILE, TILE), 1)
    causal = row >= col
    head_of_lane = lax.broadcasted_iota(jnp.int32, (TILE, GROUP_W), 1) // SSM_HEADDIM
    lb = lb_ref[0]
    hnw = hnw_ref[0]

    def tile_body(t, carry):
        r0 = pl.multiple_of(t * TILE, TILE)
        rows = pl.ds(r0, TILE)
        _hgrn_tile(rows, q_ref, f_ref, v_ref, og_ref, lb, hnw, sums_ref, mask_ref, o_ref, st_scr,
                   first)
        for gi in range(MIX_GROUPS):
            _ssd_tile(rows, r0, t, gi, x_ref, bm_ref, cm_ref, z_ref, dt_ref, cum_ref,
                      wx_ref, wb_ref, wc_ref, cbx_ref, cbb_ref, cbc_ref, dsk_ref, snw_ref, exp_ref,
                      tail_scr, y_ref, ht_scr, causal, head_of_lane)
        return carry

    lax.fori_loop(0, MIX_ROWS // TILE, tile_body, 0, unroll=2)

    last = slice(MIX_ROWS - SUBLANES, MIX_ROWS)
    b_off = MIX_GROUPS * GROUP_W
    c_off = b_off + MIX_GROUPS * SSM_STATE
    tail_scr[:, 0:b_off] = x_ref[last, :]
    tail_scr[:, b_off:c_off] = bm_ref[last, :]
    tail_scr[:, c_off:] = cm_ref[last, :]

    @pl.when(chunk == pl.num_programs(2) - 1)
    def _():
        for hh in range(HG_HB):
            s_ref[0, hh] = st_scr[hh].T
        for gi in range(MIX_GROUPS):
            h_ref[0, gi * SSM_HPG:(gi + 1) * SSM_HPG] = ht_scr[gi].T.reshape(
                SSM_HPG, SSM_HEADDIM, SSM_STATE)


def _mixer_prompt(proj, dt, cum, lbs, p, l):
    rows = MIX_ROWS
    n_chunks = SEQ // rows
    hw = HG_HB * HG_DK
    xw = MIX_GROUPS * GROUP_W
    nw = MIX_GROUPS * SSM_STATE
    b0 = COL_XBC + SSM_INNER
    c0 = b0 + SSM_GROUPS * SSM_STATE
    act = lambda width, col0: pl.BlockSpec(
        (rows, width), lambda b, h, c: (b * n_chunks + c, col0 // width + h))
    par = lambda nrows, width, col0: pl.BlockSpec(
        (1, nrows, width), lambda b, h, c: (l, 0, col0 // width + h))
    narrow = pl.BlockSpec((rows, DT_PAD), lambda b, h, c: (b * n_chunks + c, 0))
    sums = jnp.asarray(_hgrn_sum_ranges(), BF16)
    masks = jnp.asarray(_hgrn_level_masks(), F32)
    in_specs = [
        act(hw, COL_Q), act(hw, COL_F), act(hw, COL_I), act(hw, COL_OG),
        par(1, hw, 0), pl.BlockSpec((1, 1, HG_DV), lambda b, h, c: (l, 0, 0)),
        pl.BlockSpec(sums.shape, lambda b, h, c: (0, 0)),
        pl.BlockSpec(masks.shape, lambda b, h, c: (0, 0, 0)),
        act(xw, COL_XBC), act(nw, b0), act(nw, c0), act(xw, COL_Z), narrow, narrow,
        par(SSM_CONV, xw, 0), par(SSM_CONV, nw, SSM_INNER),
        par(SSM_CONV, nw, SSM_INNER + SSM_GROUPS * SSM_STATE),
        par(1, xw, 0), par(1, nw, SSM_INNER), par(1, nw, SSM_INNER + SSM_GROUPS * SSM_STATE),
        par(1, xw, 0), par(1, xw, 0),
        pl.BlockSpec((MIX_GROUPS, DT_PAD, GROUP_W), lambda b, h, c: (h, 0, 0)),
    ]
    args = (proj, proj, proj, proj, lbs, p['hg_norm'], sums, masks,
            proj, proj, proj, proj, dt, cum,
            p['conv_w'], p['conv_w'], p['conv_w'], p['conv_b'], p['conv_b'], p['conv_b'],
            p['d_skip_x'], p['ssm_norm'], p['expand_g'])

    def body(*refs):
        ins = list(refs[:len(args)])
        for i in range(14, 22):
            ins[i] = ins[i].at[0]
        _mixer_prompt_kernel(*ins, *refs[len(args):], first=(l == 0))

    return pl.pallas_call(
        body,
        grid=(BATCH, HG_HEADS // HG_HB, n_chunks),
        in_specs=in_specs,
        out_specs=[
            pl.BlockSpec((rows, hw), lambda b, h, c: (b * n_chunks + c, h)),
            pl.BlockSpec((1, HG_HB, HG_DK, HG_DV), lambda b, h, c: (b, h, 0, 0)),
            pl.BlockSpec((rows, xw), lambda b, h, c: (b * n_chunks + c, h)),
            pl.BlockSpec((1, MIX_GROUPS * SSM_HPG, SSM_HEADDIM, SSM_STATE),
                         lambda b, h, c: (b, h, 0, 0)),
        ],
        out_shape=[
            jax.ShapeDtypeStruct((BATCH * SEQ, HG_HEADS * HG_DV), BF16),
            jax.ShapeDtypeStruct((BATCH, HG_HEADS, HG_DK, HG_DV), F32),
            jax.ShapeDtypeStruct((BATCH * SEQ, SSM_INNER), BF16),
            jax.ShapeDtypeStruct((BATCH, SSM_HEADS, SSM_HEADDIM, SSM_STATE), F32),
        ],
        scratch_shapes=[
            pltpu.VMEM((HG_HB, HG_DV, HG_DK), F32),
            pltpu.VMEM((MIX_GROUPS, SSM_STATE, GROUP_W), F32),
            pltpu.VMEM((SUBLANES, xw + 2 * nw), F32),
        ],
        compiler_params=_cparams("arbitrary", "arbitrary", "arbitrary"),
        name="mixer_prompt",
    )(*args)


HG_STEP_NB = 8


def _column(mat_t, lane_mask):
    return jnp.sum(jnp.where(lane_mask, mat_t, 0.0), axis=1, keepdims=True)


def _row_group(ref, bidx, cols):
    start = pl.multiple_of((bidx // SUBLANES) * SUBLANES, SUBLANES)
    idx = (pl.ds(start, SUBLANES), cols)
    blk = ref[idx]
    is_row = lax.broadcasted_iota(jnp.int32, blk.shape, 0) == bidx % SUBLANES
    return idx, blk, is_row


def _get_row(ref, bidx, cols):
    _, blk, is_row = _row_group(ref, bidx, cols)
    return jnp.sum(jnp.where(is_row, blk, 0.0), axis=0, keepdims=True)


def _set_row(ref, bidx, cols, row):
    idx, blk, is_row = _row_group(ref, bidx, cols)
    ref[idx] = jnp.where(is_row, row, blk)


BF16_ROWS = 16


def _one_hot_rows(row, bidx):
    sub = lax.broadcasted_iota(jnp.int32, (DEC_BATCH, row.shape[1]), 0)
    return jnp.where(sub == bidx, row, 0.0).astype(BF16)


def _row_times(row, mat_nt=None, mat=None):
    lhs = jnp.broadcast_to(row.astype(BF16), (BF16_ROWS, row.shape[1]))
    out = _dot(lhs, mat) if mat is not None else _dot_nt(lhs, mat_nt)
    return out[0:1]


def _hgrn_step_kernel(q_ref, f_ref, v_ref, og_ref, lb_ref, nw_ref, s_ref, o_ref, so_ref,
                      dec_t, k_t, q_s, *, first, zero_rest):
    i = pl.program_id(0)

    @pl.when(i == 0)
    def _():
        g, k = _hgrn_gates(f_ref[...], lb_ref[0], first)
        dec = jnp.exp(g)
        q_s[...] = _silu(q_ref[...]) * (HG_DK ** -0.5)
        o_ref[...] = jnp.zeros_like(o_ref)
        for h in range(HG_HEADS):
            hs = slice(h * HG_DK, (h + 1) * HG_DK)
            dec_t[hs, :] = dec[:, hs].T
            k_t[hs, :] = k[:, hs].T.astype(BF16)

    if zero_rest:
        so_ref[1:] = jnp.zeros((DEPTH - 1,) + so_ref.shape[1:], F32)

    lane = lax.broadcasted_iota(jnp.int32, (HG_DK, DEC_BATCH), 1)

    def body(j, carry):
        bidx = i * HG_STEP_NB + j
        m = lane == bidx
        for h in range(HG_HEADS):
            hs = slice(h * HG_DK, (h + 1) * HG_DK)
            d_col = _column(dec_t[hs, :], m)
            kv = _dot(k_t[hs, :], _one_hot_rows(_get_row(v_ref, bidx, hs), bidx))
            s_new = d_col * s_ref[0, j, h] + kv
            so_ref[0, j, h] = s_new
            o_row = _row_times(_get_row(q_s, bidx, hs), mat=s_new.astype(BF16))
            _set_row(o_ref, bidx, hs, _rms(o_row, nw_ref[0]) * _silu(_get_row(og_ref, bidx, hs)))
        return carry

    lax.fori_loop(0, HG_STEP_NB, body, 0)


def _stacked_state_specs(state, nb, l, out_state):
    tail = state.shape[2:]
    zeros = (0,) * len(tail)
    in_spec = pl.BlockSpec((1, nb) + tail, lambda i: (l, i) + zeros)
    if out_state is None:
        return in_spec, pl.BlockSpec((DEPTH, nb) + tail, lambda i: (0, i) + zeros), [], [], {}
    out_spec = pl.BlockSpec((1, nb) + tail, lambda i: (l, i) + zeros)
    return in_spec, out_spec, [pl.BlockSpec(memory_space=pl.ANY)], [out_state], None


def _hgrn_step(proj, state, lbs, hg_norm, out_state, l):
    nb = HG_STEP_NB
    blk = lambda c0: pl.BlockSpec((DEC_BATCH, D_MODEL), lambda i: (0, c0 // D_MODEL))
    sspec, ospec, extra_specs, extra_args, aliases = _stacked_state_specs(state, nb, l, out_state)
    in_specs = [
        blk(COL_Q), blk(COL_F), blk(COL_I), blk(COL_OG),
        pl.BlockSpec((1, 1, D_MODEL), lambda i: (l, 0, 0)),
        pl.BlockSpec((1, 1, HG_DV), lambda i: (l, 0, 0)),
        sspec,
    ] + extra_specs
    args = [proj, proj, proj, proj, lbs, hg_norm, state] + extra_args
    if aliases is None:
        aliases = {len(args) - 1: 1}

    def body(*refs):
        ins, rest = refs[:7], refs[len(args):]
        _hgrn_step_kernel(*ins, *rest, first=(l == 0), zero_rest=(out_state is None))

    return pl.pallas_call(
        body,
        grid=(DEC_BATCH // nb,),
        in_specs=in_specs,
        out_specs=[pl.BlockSpec((DEC_BATCH, D_MODEL), lambda i: (0, 0)), ospec],
        out_shape=[
            jax.ShapeDtypeStruct((DEC_BATCH, D_MODEL), F32),
            jax.ShapeDtypeStruct(state.shape, F32),
        ],
        scratch_shapes=[pltpu.VMEM((D_MODEL, DEC_BATCH), F32), pltpu.VMEM((D_MODEL, DEC_BATCH), BF16),
                        pltpu.VMEM((DEC_BATCH, D_MODEL), F32)],
        input_output_aliases=aliases,
        compiler_params=_cparams("arbitrary"),
        name="hgrn_step",
    )(*args)


def _ssm_prep_kernel(xs_ref, bm_ref, cm_ref, dt_ref, cst_ref, w_ref, cb_ref,
                     alog_ref, exp_ref,
                     xc_ref, bc_ref, cc_ref, dec_ref, xdtt_ref, cnew_ref):
    ch = SSM_CONV_CH
    parts = ((xs_ref, 0, SSM_INNER, xc_ref),
             (bm_ref, SSM_INNER, SSM_INNER + 1024, bc_ref),
             (cm_ref, SSM_INNER + 1024, ch, cc_ref))
    for u_ref, c0, c1, out_ref in parts:
        u = u_ref[...]
        acc = cb_ref[:, c0:c1] + w_ref[SSM_CONV - 1:SSM_CONV, c0:c1] * u
        for j in range(SSM_CONV - 1):
            acc = acc + w_ref[j:j + 1, c0:c1] * cst_ref[:, j * ch + c0:j * ch + c1]
        out_ref[...] = _silu(acc)
        cnew_ref[:, (SSM_CONV - 2) * ch + c0:(SSM_CONV - 2) * ch + c1] = u
    cnew_ref[:, 0:(SSM_CONV - 2) * ch] = cst_ref[:, ch:(SSM_CONV - 1) * ch]
    dtv = _dot_sel(dt_ref[...], exp_ref[...])
    dec_ref[...] = jnp.exp(dtv * -jnp.exp(alog_ref[...]))
    xdt = xc_ref[...] * dtv
    for m in range(SSM_INNER // LANE):
        ms = slice(m * LANE, (m + 1) * LANE)
        xdtt_ref[ms, :] = xdt[:, ms].T.astype(BF16)


def _ssm_prep(proj, dt, conv_state, p, l):
    n = DEC_BATCH
    full = lambda shape: pl.BlockSpec(shape, lambda i: tuple(0 for _ in shape))
    lspec = lambda rows, width: pl.BlockSpec((1, rows, width), lambda i: (l, 0, 0))

    def body(xs, bm, cm, dtr, cst, w, cb, alog, ex, *outs):
        _ssm_prep_kernel(xs, bm, cm, dtr, cst.at[0], w.at[0], cb.at[0], alog.at[0], ex, *outs)

    return pl.pallas_call(
        body,
        grid=(1,),
        in_specs=[
            pl.BlockSpec((n, SSM_INNER), lambda i: (0, COL_XBC // SSM_INNER)),
            pl.BlockSpec((n, 1024), lambda i: (0, (COL_XBC + SSM_INNER) // 1024)),
            pl.BlockSpec((n, 1024), lambda i: (0, (COL_XBC + SSM_INNER) // 1024 + 1)),
            full((n, DT_PAD)),
            pl.BlockSpec((1, n, (SSM_CONV - 1) * SSM_CONV_CH), lambda i: (l, 0, 0)),
            lspec(SSM_CONV, SSM_CONV_CH), lspec(1, SSM_CONV_CH),
            lspec(1, SSM_INNER),
            full((DT_PAD, SSM_INNER)),
        ],
        out_specs=[
            full((n, SSM_INNER)), full((n, 1024)), full((n, 1024)),
            full((n, SSM_INNER)), full((SSM_INNER, n)),
            full((n, (SSM_CONV - 1) * SSM_CONV_CH)),
        ],
        out_shape=[
            jax.ShapeDtypeStruct((n, SSM_INNER), F32),
            jax.ShapeDtypeStruct((n, 1024), F32),
            jax.ShapeDtypeStruct((n, 1024), F32),
            jax.ShapeDtypeStruct((n, SSM_INNER), F32),
            jax.ShapeDtypeStruct((SSM_INNER, n), BF16),
            jax.ShapeDtypeStruct((n, (SSM_CONV - 1) * SSM_CONV_CH), F32),
        ],
        compiler_params=_cparams("arbitrary"),
        name="ssm_prep",
    )(proj, proj, proj, dt, conv_state, p['conv_w'], p['conv_b'], p['a_log_x'], p['expand_all'])


SSM_STEP_NB = 4


def _ssm_step_kernel(dec_ref, xdtt_ref, bc_ref, cc_ref, s_ref, y_ref, so_ref, *, zero_rest):
    i = pl.program_id(0)

    @pl.when(i == 0)
    def _():
        y_ref[...] = jnp.zeros_like(y_ref)

    if zero_rest:
        so_ref[1:] = jnp.zeros((DEPTH - 1,) + so_ref.shape[1:], F32)

    def body(j, carry):
        bidx = i * SSM_STEP_NB + j
        for g in range(SSM_GROUPS):
            rs = slice(g * GROUP_W, (g + 1) * GROUP_W)
            hs = slice(g * SSM_HPG, (g + 1) * SSM_HPG)
            ns = slice(g * SSM_STATE, (g + 1) * SSM_STATE)
            dec_row = _get_row(dec_ref, bidx, rs)
            decay = jnp.concatenate(
                [jnp.broadcast_to(dec_row[:, k * SSM_HEADDIM:k * SSM_HEADDIM + 1],
                                  (SSM_HEADDIM, SSM_STATE)) for k in range(SSM_HPG)], axis=0)
            inc = _dot(xdtt_ref[rs, :], _one_hot_rows(_get_row(bc_ref, bidx, ns), bidx))
            h = s_ref[0, j, hs].reshape(GROUP_W, SSM_STATE)
            h_new = decay * h + inc
            so_ref[0, j, hs] = h_new.reshape(SSM_HPG, SSM_HEADDIM, SSM_STATE)
            y_row = _row_times(_get_row(cc_ref, bidx, ns), mat_nt=h_new.astype(BF16))
            _set_row(y_ref, bidx, rs, y_row)
        return carry

    lax.fori_loop(0, SSM_STEP_NB, body, 0)


def _ssm_step(dec, xdtt, bc, cc, state, out_state, l):
    nb = SSM_STEP_NB
    full = lambda shape: pl.BlockSpec(shape, lambda i: tuple(0 for _ in shape))
    sspec, ospec, extra_specs, extra_args, aliases = _stacked_state_specs(state, nb, l, out_state)
    in_specs = [full(dec.shape), full(xdtt.shape), full(bc.shape), full(cc.shape),
                sspec] + extra_specs
    args = [dec, xdtt, bc, cc, state] + extra_args
    if aliases is None:
        aliases = {len(args) - 1: 1}

    def body(*refs):
        ins, rest = refs[:5], refs[len(args):]
        _ssm_step_kernel(*ins, *rest, zero_rest=(out_state is None))

    return pl.pallas_call(
        body,
        grid=(DEC_BATCH // nb,),
        in_specs=in_specs,
        out_specs=[full((DEC_BATCH, SSM_INNER)), ospec],
        out_shape=[
            jax.ShapeDtypeStruct((DEC_BATCH, SSM_INNER), F32),
            jax.ShapeDtypeStruct(state.shape, F32),
        ],
        input_output_aliases=aliases,
        compiler_params=_cparams("arbitrary"),
        name="ssm_step",
    )(*args)


def _ssm_post_kernel(y_in_ref, xc_ref, z_ref, dsk_ref, nw_ref, y_ref):
    for g in range(SSM_GROUPS):
        gs = slice(g * GROUP_W, (g + 1) * GROUP_W)
        y = (y_in_ref[:, gs] + dsk_ref[:, gs] * xc_ref[:, gs]) * _silu(z_ref[:, gs])
        y_ref[:, gs] = _rms(y, nw_ref[:, gs])


def _ssm_post(y, xc, proj, p, l):
    n = DEC_BATCH
    full = lambda shape: pl.BlockSpec(shape, lambda i: tuple(0 for _ in shape))
    lspec = pl.BlockSpec((1, 1, SSM_INNER), lambda i: (l, 0, 0))

    def body(y_in, xc_r, z_r, dsk, nw, y_r):
        _ssm_post_kernel(y_in, xc_r, z_r, dsk.at[0], nw.at[0], y_r)

    return pl.pallas_call(
        body,
        grid=(1,),
        in_specs=[full((n, SSM_INNER)), full((n, SSM_INNER)),
                  pl.BlockSpec((n, SSM_INNER), lambda i: (0, COL_Z // SSM_INNER)),
                  lspec, lspec],
        out_specs=full((n, SSM_INNER)),
        out_shape=jax.ShapeDtypeStruct((n, SSM_INNER), F32),
        compiler_params=_cparams("arbitrary"),
        name="ssm_post",
    )(y, xc, proj, p['d_skip_x'], p['ssm_norm'])


def _merge_kernel(x_ref, oh_ref, y_ref, ga_ref, gb_ref, g1_ref, bma_ref, bmb_ref,
                  wa_ref, wb_ref, wo_ref, o_ref):
    a = _dot(oh_ref[...].astype(BF16), wa_ref[0])
    b = _dot(y_ref[...].astype(BF16), wb_ref[0])
    u = _sigmoid(ga_ref[...] + bma_ref[0]) * a + _sigmoid(gb_ref[...] + bmb_ref[0]) * b
    o_ref[...] = x_ref[...] + g1_ref[0] * _dot(u.astype(BF16), wo_ref[0])


def _merge(x, o_hg, y, proj, mod, p, l, tm, rows_per_group):
    rows = x.shape[0]
    wspec = lambda k: pl.BlockSpec((1, k, D_MODEL), lambda i: (l, 0, 0))
    return pl.pallas_call(
        _merge_kernel,
        grid=(rows // tm,),
        in_specs=[
            pl.BlockSpec((tm, D_MODEL), lambda i: (i, 0)),
            pl.BlockSpec((tm, D_MODEL), lambda i: (i, 0)),
            pl.BlockSpec((tm, SSM_INNER), lambda i: (i, 0)),
            pl.BlockSpec((tm, D_MODEL), lambda i: (i, COL_GA // D_MODEL)),
            pl.BlockSpec((tm, D_MODEL), lambda i: (i, COL_GB // D_MODEL)),
            _mod_spec(mod, tm, rows_per_group, ADA_G1),
            pl.BlockSpec((1, 1, D_MODEL), lambda i: (l, 0, 0)),
            pl.BlockSpec((1, 1, D_MODEL), lambda i: (l, 0, 1)),
            wspec(D_MODEL), wspec(SSM_INNER), wspec(D_MODEL),
        ],
        out_specs=pl.BlockSpec((tm, D_MODEL), lambda i: (i, 0)),
        out_shape=jax.ShapeDtypeStruct((rows, D_MODEL), F32),
        compiler_params=_cparams("arbitrary"),
        name="merge",
    )(x, o_hg, y, proj, proj, mod, p['b_merge'], p['b_merge'], p['w_br_a'], p['w_br_b'],
      p['w_out'])


FF_CHUNK = 1024


def _mlp_kernel(x_ref, sc_ref, sh_ref, g_ref, nw_ref, wu_ref, wd_ref, fw_ref, o_ref, *, final):
    x = x_ref[...]
    h = (_rms(x, nw_ref[0]) * (1.0 + sc_ref[0]) + sh_ref[0]).astype(BF16)
    acc = jnp.zeros(x.shape, F32)
    for c in range(D_FF // FF_CHUNK):
        cs = slice(c * FF_CHUNK, (c + 1) * FF_CHUNK)
        a = jnp.maximum(_dot(h, wu_ref[0, :, cs]), 0.0)
        acc = acc + _dot((a * a).astype(BF16), wd_ref[0, cs, :])
    out = x + g_ref[0] * acc
    if final:
        out = _rms(out, fw_ref[...])
    o_ref[...] = out


def _mlp(x, mod, p, norm_final, l, tm, rows_per_group):
    rows = x.shape[0]
    return pl.pallas_call(
        functools.partial(_mlp_kernel, final=(l == DEPTH - 1)),
        grid=(rows // tm,),
        in_specs=[
            pl.BlockSpec((tm, D_MODEL), lambda i: (i, 0)),
            _mod_spec(mod, tm, rows_per_group, ADA_SC2),
            _mod_spec(mod, tm, rows_per_group, ADA_SH2),
            _mod_spec(mod, tm, rows_per_group, ADA_G2),
            pl.BlockSpec((1, 1, D_MODEL), lambda i: (l, 0, 0)),
            pl.BlockSpec((1, D_MODEL, D_FF), lambda i: (l, 0, 0)),
            pl.BlockSpec((1, D_FF, D_MODEL), lambda i: (l, 0, 0)),
            pl.BlockSpec((1, D_MODEL), lambda i: (0, 0)),
        ],
        out_specs=pl.BlockSpec((tm, D_MODEL), lambda i: (i, 0)),
        out_shape=jax.ShapeDtypeStruct((rows, D_MODEL), F32),
        compiler_params=_cparams("arbitrary"),
        name="mlp",
    )(x, mod, mod, mod, p['norm_mlp'], p['w_up'], p['w_down'], norm_final)


def _prepare_params(w_ada, b_ada, norm_mix, w_in, b_merge, hg_norm, conv_w, conv_b, dt_bias,
                    a_log, d_skip, ssm_norm, w_br_a, w_br_b, w_out, norm_mlp, w_up, w_down):
    dt0 = COL_XBC + SSM_CONV_CH
    w_main = jnp.concatenate([w_in[:, :, :dt0], w_in[:, :, dt0 + SSM_HEADS:]], axis=-1)
    w_dt = jnp.pad(w_in[:, :, dt0:dt0 + SSM_HEADS], ((0, 0), (0, 0), (0, DT_PAD - SSM_HEADS)))
    per_head = lambda a: jnp.repeat(a, SSM_HEADDIM, axis=-1).reshape(DEPTH, 1, SSM_INNER)
    narrow = lambda a: jnp.pad(a, ((0, 0), (0, DT_PAD - SSM_HEADS))).reshape(DEPTH, 1, DT_PAD)
    head_of_col = jnp.arange(SSM_INNER) // SSM_HEADDIM
    expand_all = (jnp.arange(DT_PAD)[:, None] == head_of_col[None, :]).astype(BF16)
    expand_g = expand_all.reshape(DT_PAD, SSM_GROUPS, GROUP_W).transpose(1, 0, 2)
    return {
        'norm_mix': norm_mix.reshape(DEPTH, 1, D_MODEL),
        'w_main': w_main.astype(BF16), 'w_dt': w_dt.astype(BF16),
        'b_merge': b_merge.reshape(DEPTH, 1, 2 * D_MODEL),
        'hg_norm': hg_norm.reshape(DEPTH, 1, HG_DV),
        'conv_w': conv_w, 'conv_b': conv_b.reshape(DEPTH, 1, SSM_CONV_CH),
        'dt_bias_n': narrow(dt_bias), 'a_log_n': narrow(a_log),
        'a_log_x': per_head(a_log), 'd_skip_x': per_head(d_skip),
        'ssm_norm': ssm_norm.reshape(DEPTH, 1, SSM_INNER),
        'expand_all': expand_all, 'expand_g': expand_g,
        'w_br_a': w_br_a.astype(BF16), 'w_br_b': w_br_b.astype(BF16),
        'w_out': w_out.astype(BF16),
        'norm_mlp': norm_mlp.reshape(DEPTH, 1, D_MODEL),
        'w_up': w_up.astype(BF16), 'w_down': w_down.astype(BF16),
    }


def kernel(x_prompt, x_sample, state_hgrn, state_ssm, state_conv, c_prompt, c_sample, w_ada, b_ada, norm_mix, w_in, b_merge, lower_bounds, hg_norm, conv_w, conv_b, dt_bias, a_log, d_skip, ssm_norm, w_br_a, w_br_b, w_out, norm_mlp, w_up, w_down, norm_final):
    p = _prepare_params(w_ada, b_ada, norm_mix, w_in, b_merge, hg_norm, conv_w, conv_b, dt_bias,
                        a_log, d_skip, ssm_norm, w_br_a, w_br_b, w_out, norm_mlp, w_up, w_down)
    lbs = _lower_bounds(lower_bounds).reshape(DEPTH, 1, HG_HEADS * HG_DK)
    mod = _modulation(jnp.concatenate([c_prompt, c_sample], axis=0), w_ada, b_ada)
    mod_p = mod[:, :BATCH].reshape(DEPTH, BATCH, 1, N_ADA * D_MODEL)
    mod_s = mod[:, BATCH:].reshape(DEPTH, 1, DEC_BATCH, N_ADA * D_MODEL)
    norm_final2 = norm_final.reshape(1, D_MODEL)
    conv_state = state_conv.reshape(DEPTH, DEC_BATCH, (SSM_CONV - 1) * SSM_CONV_CH)

    xp = x_prompt.reshape(BATCH * SEQ, D_MODEL)
    xs = x_sample.reshape(DEC_BATCH, D_MODEL)
    hg_p, ssm_p, cv_p, cv_s = [], [], [], []
    hg_s = ssm_s = None
    for l in range(DEPTH):
        proj, dt, cum = _inproj(xp, mod_p[l], p, l, SEQ, SEQ, 1)
        o_hg, s_new, y, h_new = _mixer_prompt(proj, dt, cum, lbs, p, l)
        hg_p.append(s_new)
        ssm_p.append(h_new)
        cv_p.append(proj.reshape(BATCH, SEQ, PROJ_W)[:, SEQ - (SSM_CONV - 1):,
                                                       COL_XBC:COL_XBC + SSM_CONV_CH])
        xp = _merge(xp, o_hg, y, proj, mod_p[l], p, l, 512, SEQ)
        xp = _mlp(xp, mod_p[l], p, norm_final2, l, 512, SEQ)
        proj, dt, _ = _inproj(xs, mod_s[l], p, l, DEC_BATCH, DEC_BATCH, 3)
        o_hg, hg_s = _hgrn_step(proj, state_hgrn, lbs, p['hg_norm'], hg_s, l)
        xc, bc, cc, dect, xdtt, cnew = _ssm_prep(proj, dt, conv_state, p, l)
        yt, ssm_s = _ssm_step(dect, xdtt, bc, cc, state_ssm, ssm_s, l)
        y = _ssm_post(yt, xc, proj, p, l)
        cv_s.append(cnew.reshape(DEC_BATCH, SSM_CONV - 1, SSM_CONV_CH))
        xs = _merge(xs, o_hg, y, proj, mod_s[l], p, l, DEC_BATCH, DEC_BATCH)
        xs = _mlp(xs, mod_s[l], p, norm_final2, l, DEC_BATCH, DEC_BATCH)
    return (xp.reshape(BATCH, SEQ, D_MODEL), xs.reshape(DEC_BATCH, 1, D_MODEL),
            jnp.stack(hg_p), jnp.stack(ssm_p), jnp.stack(cv_p),
            hg_s, ssm_s, jnp.stack(cv_s))
```

```python
import functools

import jax
import jax.numpy as jnp
import numpy as np
from jax import lax
from jax.experimental import pallas as pl
from jax.experimental.pallas import tpu as pltpu

F32 = jnp.float32
BF16 = jnp.bfloat16

D_MODEL = 1024
BATCH = 8
SEQ = 2048
DEPTH = 4
DEC_BATCH = 128
HG_HEADS = 8
HG_DK = 128
HG_DV = 128
SSM_INNER = 2048
SSM_HEADDIM = 64
SSM_HEADS = 32
SSM_GROUPS = 8
SSM_HPG = 4
SSM_STATE = 128
SSM_CONV = 4
SSM_CONV_CH = 4096
GROUP_W = SSM_HPG * SSM_HEADDIM
D_FF = 4096
N_ADA = 6
EPS = 1e-6
DT_PAD = 128
COL_Q, COL_F, COL_I, COL_OG = 0, 1024, 2048, 3072
COL_Z, COL_XBC, COL_GA, COL_GB = 4096, 6144, 10240, 11264
PROJ_W = 12288
ADA_SH1, ADA_SC1, ADA_G1, ADA_SH2, ADA_SC2, ADA_G2 = range(6)

LANE = 128
SUBLANES = 8
TILE = 128
VMEM_LIMIT = 56 * 1024 * 1024


def _cparams(*sem):
    return pltpu.CompilerParams(dimension_semantics=sem, vmem_limit_bytes=VMEM_LIMIT)


def _sigmoid(x):
    return 0.5 * jnp.tanh(0.5 * x) + 0.5


def _silu(x):
    return x * _sigmoid(x)


def _softplus(x):
    return jnp.maximum(x, 0.0) + jnp.log1p(jnp.exp(-jnp.abs(x)))


def _dot(a, b):
    return jnp.dot(a, b, preferred_element_type=F32)


def _dot_nt(a, b):
    return lax.dot_general(a, b, (((1,), (1,)), ((), ())), preferred_element_type=F32)


def _split(a, terms):
    parts = []
    for _ in range(terms - 1):
        hi = a.astype(BF16)
        parts.append(hi)
        a = a - hi.astype(F32)
    parts.append(a.astype(BF16))
    return parts


def _sel_dot(sel, a, terms=3):
    return _dot(jnp.concatenate([sel] * terms, axis=1), jnp.concatenate(_split(a, terms), axis=0))


def _dot_sel(a, sel, terms=3):
    return _dot(jnp.concatenate(_split(a, terms), axis=1), jnp.concatenate([sel] * terms, axis=0))


def _rms(x, w):
    ms = jnp.mean(x * x, axis=-1, keepdims=True)
    return x * lax.rsqrt(ms + EPS) * w


def _hgrn_gates(f, lb, first):
    if first:
        g = jnp.minimum(f, 0.0) - jnp.log1p(jnp.exp(-jnp.abs(f)))
        k = _sigmoid(-f)
    else:
        fg = lb + (1.0 - lb) * _sigmoid(f)
        g = jnp.log(fg)
        k = 1.0 - fg
    return g, k


def _lbs_kernel(lb_ref, o_ref):
    x = lb_ref[...]
    m = jnp.max(x, axis=0, keepdims=True)
    e = jnp.exp(x - m)
    p = e / jnp.sum(e, axis=0, keepdims=True)
    acc = jnp.zeros_like(p[0:1])
    rows = [acc]
    for l in range(1, DEPTH):
        acc = acc + p[l:l + 1]
        rows.append(acc)
    o_ref[...] = jnp.concatenate(rows, axis=0)


def _lower_bounds(lower_bounds):
    return pl.pallas_call(
        _lbs_kernel, out_shape=jax.ShapeDtypeStruct(lower_bounds.shape, F32), name="lbs",
    )(lower_bounds)


MOD_STREAMS = 3


def _mod_kernel(c_ref, *refs):
    w_refs, b_ref, o_ref = refs[:MOD_STREAMS], refs[MOD_STREAMS], refs[MOD_STREAMS + 1]
    a = _silu(c_ref[...]).astype(BF16)
    for k, w_ref in enumerate(w_refs):
        cs = slice(k * D_MODEL, (k + 1) * D_MODEL)
        o_ref[0, :, cs] = _dot(a, w_ref[0].astype(BF16)) + b_ref[0, :, cs]


def _modulation(c_all, w_ada, b_ada):
    n = c_all.shape[0]
    tn = MOD_STREAMS * D_MODEL
    w_spec = lambda k: pl.BlockSpec((1, D_MODEL, D_MODEL), lambda l, j: (l, 0, j * MOD_STREAMS + k))
    return pl.pallas_call(
        _mod_kernel,
        grid=(DEPTH, N_ADA * D_MODEL // tn),
        in_specs=[pl.BlockSpec((n, D_MODEL), lambda l, j: (0, 0))]
        + [w_spec(k) for k in range(MOD_STREAMS)]
        + [pl.BlockSpec((1, 1, tn), lambda l, j: (l, 0, j))],
        out_specs=pl.BlockSpec((1, n, tn), lambda l, j: (l, 0, j)),
        out_shape=jax.ShapeDtypeStruct((DEPTH, n, N_ADA * D_MODEL), F32),
        compiler_params=_cparams("arbitrary", "arbitrary"),
        name="modulation",
    )(c_all, *([w_ada] * MOD_STREAMS), b_ada.reshape(DEPTH, 1, N_ADA * D_MODEL))


def _mod_spec(mod, tm, rows_per_group, piece):
    if mod.shape[1] == 1:
        tiles = rows_per_group // tm
        return pl.BlockSpec((1, 1, D_MODEL), lambda i, *_: (i // tiles, 0, piece))
    return pl.BlockSpec((1, tm, D_MODEL), lambda i, *_: (0, i, piece))


def _inproj_kernel(x_ref, sc_ref, sh_ref, nw_ref, *refs, streams):
    w_refs = refs[:streams]
    wdt_ref, dtb_ref, alog_ref, o_ref, odt_ref, ocum_ref, h_scr = refs[streams:]

    @pl.when(pl.program_id(1) == 0)
    def _():
        h = _rms(x_ref[...], nw_ref[0]) * (1.0 + sc_ref[0]) + sh_ref[0]
        hb = h.astype(BF16)
        h_scr[...] = hb
        dtv = _softplus(_dot(hb, wdt_ref[0]) + dtb_ref[0])
        odt_ref[...] = dtv
        la = dtv * -jnp.exp(alog_ref[0])
        row = lax.broadcasted_iota(jnp.int32, (TILE, TILE), 0)
        col = lax.broadcasted_iota(jnp.int32, (TILE, TILE), 1)
        tri = jnp.where(row >= col, 1.0, 0.0).astype(BF16)
        for i in range(la.shape[0] // TILE):
            rs = slice(i * TILE, (i + 1) * TILE)
            ocum_ref[rs, :] = _sel_dot(tri, la[rs] * LOG2E)

    h = h_scr[...]
    for k, w_ref in enumerate(w_refs):
        o_ref[:, k * INPROJ_TN:(k + 1) * INPROJ_TN] = _dot(h, w_ref[0])


INPROJ_TN = 1024


def _inproj(x, mod, p, l, tm, rows_per_group, streams):
    rows = x.shape[0]
    tn = streams * INPROJ_TN
    narrow = pl.BlockSpec((1, 1, DT_PAD), lambda i, j: (l, 0, 0))
    w_spec = lambda k: pl.BlockSpec((1, D_MODEL, INPROJ_TN), lambda i, j: (l, 0, j * streams + k))
    return pl.pallas_call(
        functools.partial(_inproj_kernel, streams=streams),
        grid=(rows // tm, PROJ_W // tn),
        in_specs=[
            pl.BlockSpec((tm, D_MODEL), lambda i, j: (i, 0)),
            _mod_spec(mod, tm, rows_per_group, ADA_SC1),
            _mod_spec(mod, tm, rows_per_group, ADA_SH1),
            pl.BlockSpec((1, 1, D_MODEL), lambda i, j: (l, 0, 0)),
        ] + [w_spec(k) for k in range(streams)] + [
            pl.BlockSpec((1, D_MODEL, DT_PAD), lambda i, j: (l, 0, 0)),
            narrow, narrow,
        ],
        out_specs=[
            pl.BlockSpec((tm, tn), lambda i, j: (i, j)),
            pl.BlockSpec((tm, DT_PAD), lambda i, j: (i, 0)),
            pl.BlockSpec((tm, DT_PAD), lambda i, j: (i, 0)),
        ],
        out_shape=[
            jax.ShapeDtypeStruct((rows, PROJ_W), F32),
            jax.ShapeDtypeStruct((rows, DT_PAD), F32),
            jax.ShapeDtypeStruct((rows, DT_PAD), F32),
        ],
        scratch_shapes=[pltpu.VMEM((tm, D_MODEL), BF16)],
        compiler_params=_cparams("arbitrary", "arbitrary"),
        name="inproj",
    )(x, mod, mod, p['norm_mix'], *([p['w_main']] * streams), p['w_dt'], p['dt_bias_n'],
      p['a_log_n'])


HG_LEVELS = 7
HG_HB = 4
HG_SPLIT = 2
LOG2E = 1.4426950408889634


def _hgrn_sum_ranges():
    r = np.arange(TILE)[:, None]
    c = np.arange(TILE)[None, :]
    mats = [c <= r, c > r]
    for s in range(HG_LEVELS):
        h = 1 << s
        right = ((r >> s) & 1) == 1
        in_right = (c >= (r & ~(h - 1))) & (c <= r)
        in_left = (c > r) & (c <= (r | (h - 1)))
        mats.append(np.where(right, in_right, in_left))
    m = np.concatenate(mats, axis=0).astype(np.float32)
    return np.tile(m, (1, HG_SPLIT))


def _hgrn_level_masks():
    r = np.arange(TILE)[:, None]
    c = np.arange(TILE)[None, :]
    x = r ^ c
    top = np.where(x > 0, np.floor(np.log2(np.maximum(x, 1))).astype(np.int64), -1)
    masks = [r == c] + [(r > c) & (top == s) for s in range(HG_LEVELS)]
    return np.stack(masks).astype(np.float32)


def _hgrn_tile(rows, q_ref, f_ref, v_ref, og_ref, lb, nw, sums_ref, mask_ref, o_ref, st_scr, first):
    g, k = _hgrn_gates(f_ref[rows, :], lb, first)
    g2 = jnp.concatenate(_split(g * LOG2E, HG_SPLIT), axis=0)
    decay = jnp.exp2(_dot(sums_ref[...], g2))
    q = _silu(q_ref[rows, :]) * (HG_DK ** -0.5)
    v = v_ref[rows, :]
    og = _silu(og_ref[rows, :])
    q_in = (q * decay[0:TILE]).astype(BF16)
    k_out = (k * decay[TILE:2 * TILE]).astype(BF16)
    end = decay[TILE - 1:TILE]
    qb = q.astype(BF16)
    kb = k.astype(BF16)
    vb = v.astype(BF16)
    outs = []
    for hh in range(HG_HB):
        cs = slice(hh * HG_DK, (hh + 1) * HG_DK)
        st = st_scr[hh]
        p = mask_ref[0] * _dot_nt(qb[:, cs], kb[:, cs])
        for s in range(HG_LEVELS):
            a = decay[(2 + s) * TILE:(3 + s) * TILE, cs].astype(BF16)
            p = p + mask_ref[1 + s] * _dot_nt(qb[:, cs] * a, kb[:, cs] * a)
        o = _dot_nt(q_in[:, cs], st.astype(BF16)) + _dot(p.astype(BF16), vb[:, cs])
        st_scr[hh] = st * end[:, cs] + _dot(v[:, cs].T.astype(BF16), k_out[:, cs])
        outs.append(_rms(o, nw) * og[:, cs])
    o_ref[rows, :] = jnp.concatenate(outs, axis=1).astype(o_ref.dtype)


def _conv_silu_tile(ref, cols, w_ref, b_ref, tail, r0, t):
    cur = ref[pl.ds(r0, TILE), cols]
    prev = ref[pl.ds(pl.multiple_of(jnp.maximum(r0 - SUBLANES, 0), SUBLANES), SUBLANES), cols]
    prev = jnp.where(t > 0, prev, tail)
    ext = jnp.concatenate([prev, cur], axis=0)
    acc = b_ref[:, cols] + w_ref[SSM_CONV - 1:SSM_CONV, cols] * cur
    for j in range(SSM_CONV - 1):
        acc = acc + w_ref[j:j + 1, cols] * ext[5 + j:5 + j + TILE, :]
    return _silu(acc)


def _ssd_tile(rows, r0, t, gi, x_ref, bm_ref, cm_ref, z_ref, dt_ref, cum_ref,
              wx_ref, wb_ref, wc_ref, cbx_ref, cbb_ref, cbc_ref, dsk_ref, nw_ref, exp_ref,
              tail_scr, y_ref, ht_scr, causal, head_of_lane):
    xs = slice(gi * GROUP_W, (gi + 1) * GROUP_W)
    ns = slice(gi * SSM_STATE, (gi + 1) * SSM_STATE)
    b_off = MIX_GROUPS * GROUP_W
    c_off = b_off + MIX_GROUPS * SSM_STATE
    tail_b = slice(b_off + gi * SSM_STATE, b_off + (gi + 1) * SSM_STATE)
    tail_c = slice(c_off + gi * SSM_STATE, c_off + (gi + 1) * SSM_STATE)
    xc = _conv_silu_tile(x_ref, xs, wx_ref, cbx_ref, tail_scr[:, xs], r0, t)
    bc = _conv_silu_tile(bm_ref, ns, wb_ref, cbb_ref, tail_scr[:, tail_b], r0, t)
    cc = _conv_silu_tile(cm_ref, ns, wc_ref, cbc_ref, tail_scr[:, tail_c], r0, t)
    expand = exp_ref[gi]
    dtv = _dot_sel(dt_ref[rows, :], expand, terms=2)
    cum = _dot_sel(cum_ref[rows, :], expand)
    cum_end = cum[TILE - 1:TILE, :]
    cum_t = cum.T
    xdt = xc * dtv
    xdtb = xdt.astype(BF16)
    ccb = cc.astype(BF16)
    cb = jnp.where(causal, _dot_nt(ccb, bc.astype(BF16)), 0.0)
    ht = ht_scr[gi]
    weights, values = [], []
    for k in range(SSM_HPG):
        c_col = cum[:, k * SSM_HEADDIM:k * SSM_HEADDIM + 1]
        c_row = cum_t[k * SSM_HEADDIM:k * SSM_HEADDIM + 1, :]
        decay = jnp.exp2(jnp.minimum(c_col - c_row, 0.0))
        weights.append((cb * decay).astype(BF16))
        values.append(jnp.where(head_of_lane == k, xdtb, jnp.zeros_like(xdtb)))
    y = (_dot(ccb, ht.astype(BF16)) * jnp.exp2(cum)
         + _dot(jnp.concatenate(weights, axis=1), jnp.concatenate(values, axis=0)))
    x_out = (xdt * jnp.exp2(cum_end - cum)).astype(BF16)
    ht_scr[gi] = ht * jnp.exp2(cum_end) + _dot(bc.T.astype(BF16), x_out)
    y = (y + dsk_ref[:, xs] * xc) * _silu(z_ref[rows, xs])
    y_ref[rows, xs] = _rms(y, nw_ref[:, xs]).astype(y_ref.dtype)


MIX_ROWS = 512
MIX_GROUPS = SSM_GROUPS // (HG_HEADS // HG_HB)


def _mixer_prompt_kernel(q_ref, f_ref, v_ref, og_ref, lb_ref, hnw_ref, sums_ref, mask_ref,
                         x_ref, bm_ref, cm_ref, z_ref, dt_ref, cum_ref,
                         wx_ref, wb_ref, wc_ref, cbx_ref, cbb_ref, cbc_ref,
                         dsk_ref, snw_ref, exp_ref,
                         o_ref, s_ref, y_ref, h_ref, st_scr, ht_scr, tail_scr, *, first):
    chunk = pl.program_id(2)

    @pl.when(chunk == 0)
    def _():
        st_scr[...] = jnp.zeros_like(st_scr)
        ht_scr[...] = jnp.zeros_like(ht_scr)
        tail_scr[...] = jnp.zeros_like(tail_scr)

    row = lax.broadcasted_iota(jnp.int32, (TILE, TILE), 0)
    col = lax.broadcasted_iota(jnp.int32, (TILE, TILE), 1)
    causal = row >= col
    head_of_lane = lax.broadcasted_iota(jnp.int32, (TILE, GROUP_W), 1) // SSM_HEADDIM
    lb = lb_ref[0]
    hnw = hnw_ref[0]

    def tile_body(t, carry):
        r0 = pl.multiple_of(t * TILE, TILE)
        rows = pl.ds(r0, TILE)
        _hgrn_tile(rows, q_ref, f_ref, v_ref, og_ref, lb, hnw, sums_ref, mask_ref, o_ref, st_scr,
                   first)
        for gi in range(MIX_GROUPS):
            _ssd_tile(rows, r0, t, gi, x_ref, bm_ref, cm_ref, z_ref, dt_ref, cum_ref,
                      wx_ref, wb_ref, wc_ref, cbx_ref, cbb_ref, cbc_ref, dsk_ref, snw_ref, exp_ref,
                      tail_scr, y_ref, ht_scr, causal, head_of_lane)
        return carry

    lax.fori_loop(0, MIX_ROWS // TILE, tile_body, 0, unroll=2)

    last = slice(MIX_ROWS - SUBLANES, MIX_ROWS)
    b_off = MIX_GROUPS * GROUP_W
    c_off = b_off + MIX_GROUPS * SSM_STATE
    tail_scr[:, 0:b_off] = x_ref[last, :]
    tail_scr[:, b_off:c_off] = bm_ref[last, :]
    tail_scr[:, c_off:] = cm_ref[last, :]

    @pl.when(chunk == pl.num_programs(2) - 1)
    def _():
        for hh in range(HG_HB):
            s_ref[0, hh] = st_scr[hh].T
        for gi in range(MIX_GROUPS):
            h_ref[0, gi * SSM_HPG:(gi + 1) * SSM_HPG] = ht_scr[gi].T.reshape(
                SSM_HPG, SSM_HEADDIM, SSM_STATE)


def _mixer_prompt(proj, dt, cum, lbs, p, l):
    rows = MIX_ROWS
    n_chunks = SEQ // rows
    hw = HG_HB * HG_DK
    xw = MIX_GROUPS * GROUP_W
    nw = MIX_GROUPS * SSM_STATE
    b0 = COL_XBC + SSM_INNER
    c0 = b0 + SSM_GROUPS * SSM_STATE
    act = lambda width, col0: pl.BlockSpec(
        (rows, width), lambda b, h, c: (b * n_chunks + c, col0 // width + h))
    par = lambda nrows, width, col0: pl.BlockSpec(
        (1, nrows, width), lambda b, h, c: (l, 0, col0 // width + h))
    narrow = pl.BlockSpec((rows, DT_PAD), lambda b, h, c: (b * n_chunks + c, 0))
    sums = jnp.asarray(_hgrn_sum_ranges(), BF16)
    masks = jnp.asarray(_hgrn_level_masks(), F32)
    in_specs = [
        act(hw, COL_Q), act(hw, COL_F), act(hw, COL_I), act(hw, COL_OG),
        par(1, hw, 0), pl.BlockSpec((1, 1, HG_DV), lambda b, h, c: (l, 0, 0)),
        pl.BlockSpec(sums.shape, lambda b, h, c: (0, 0)),
        pl.BlockSpec(masks.shape, lambda b, h, c: (0, 0, 0)),
        act(xw, COL_XBC), act(nw, b0), act(nw, c0), act(xw, COL_Z), narrow, narrow,
        par(SSM_CONV, xw, 0), par(SSM_CONV, nw, SSM_INNER),
        par(SSM_CONV, nw, SSM_INNER + SSM_GROUPS * SSM_STATE),
        par(1, xw, 0), par(1, nw, SSM_INNER), par(1, nw, SSM_INNER + SSM_GROUPS * SSM_STATE),
        par(1, xw, 0), par(1, xw, 0),
        pl.BlockSpec((MIX_GROUPS, DT_PAD, GROUP_W), lambda b, h, c: (h, 0, 0)),
    ]
    args = (proj, proj, proj, proj, lbs, p['hg_norm'], sums, masks,
            proj, proj, proj, proj, dt, cum,
            p['conv_w'], p['conv_w'], p['conv_w'], p['conv_b'], p['conv_b'], p['conv_b'],
            p['d_skip_x'], p['ssm_norm'], p['expand_g'])

    def body(*refs):
        ins = list(refs[:len(args)])
        for i in range(14, 22):
            ins[i] = ins[i].at[0]
        _mixer_prompt_kernel(*ins, *refs[len(args):], first=(l == 0))

    return pl.pallas_call(
        body,
        grid=(BATCH, HG_HEADS // HG_HB, n_chunks),
        in_specs=in_specs,
        out_specs=[
            pl.BlockSpec((rows, hw), lambda b, h, c: (b * n_chunks + c, h)),
            pl.BlockSpec((1, HG_HB, HG_DK, HG_DV), lambda b, h, c: (b, h, 0, 0)),
            pl.BlockSpec((rows, xw), lambda b, h, c: (b * n_chunks + c, h)),
            pl.BlockSpec((1, MIX_GROUPS * SSM_HPG, SSM_HEADDIM, SSM_STATE),
                         lambda b, h, c: (b, h, 0, 0)),
        ],
        out_shape=[
            jax.ShapeDtypeStruct((BATCH * SEQ, HG_HEADS * HG_DV), BF16),
            jax.ShapeDtypeStruct((BATCH, HG_HEADS, HG_DK, HG_DV), F32),
            jax.ShapeDtypeStruct((BATCH * SEQ, SSM_INNER), BF16),
            jax.ShapeDtypeStruct((BATCH, SSM_HEADS, SSM_HEADDIM, SSM_STATE), F32),
        ],
        scratch_shapes=[
            pltpu.VMEM((HG_HB, HG_DV, HG_DK), F32),
            pltpu.VMEM((MIX_GROUPS, SSM_STATE, GROUP_W), F32),
            pltpu.VMEM((SUBLANES, xw + 2 * nw), F32),
        ],
        compiler_params=_cparams("arbitrary", "arbitrary", "arbitrary"),
        name="mixer_prompt",
    )(*args)


HG_STEP_NB = 8


def _column(mat_t, lane_mask):
    return jnp.sum(jnp.where(lane_mask, mat_t, 0.0), axis=1, keepdims=True)


def _row_group(ref, bidx, cols):
    start = pl.multiple_of((bidx // SUBLANES) * SUBLANES, SUBLANES)
    idx = (pl.ds(start, SUBLANES), cols)
    blk = ref[idx]
    is_row = lax.broadcasted_iota(jnp.int32, blk.shape, 0) == bidx % SUBLANES
    return idx, blk, is_row


def _get_row(ref, bidx, cols):
    _, blk, is_row = _row_group(ref, bidx, cols)
    return jnp.sum(jnp.where(is_row, blk, 0.0), axis=0, keepdims=True)


def _set_row(ref, bidx, cols, row):
    idx, blk, is_row = _row_group(ref, bidx, cols)
    ref[idx] = jnp.where(is_row, row, blk)


BF16_ROWS = 16


def _one_hot_rows(row, bidx):
    sub = lax.broadcasted_iota(jnp.int32, (DEC_BATCH, row.shape[1]), 0)
    return jnp.where(sub == bidx, row, 0.0).astype(BF16)


def _row_times(row, mat_nt=None, mat=None):
    lhs = jnp.broadcast_to(row.astype(BF16), (BF16_ROWS, row.shape[1]))
    out = _dot(lhs, mat) if mat is not None else _dot_nt(lhs, mat_nt)
    return out[0:1]


def _hgrn_step_kernel(q_ref, f_ref, v_ref, og_ref, lb_ref, nw_ref, s_ref, o_ref, so_ref,
                      dec_t, k_t, q_s, *, first, zero_rest):
    i = pl.program_id(0)

    @pl.when(i == 0)
    def _():
        g, k = _hgrn_gates(f_ref[...], lb_ref[0], first)
        dec = jnp.exp(g)
        q_s[...] = _silu(q_ref[...]) * (HG_DK ** -0.5)
        o_ref[...] = jnp.zeros_like(o_ref)
        for h in range(HG_HEADS):
            hs = slice(h * HG_DK, (h + 1) * HG_DK)
            dec_t[hs, :] = dec[:, hs].T
            k_t[hs, :] = k[:, hs].T.astype(BF16)

    if zero_rest:
        so_ref[1:] = jnp.zeros((DEPTH - 1,) + so_ref.shape[1:], F32)

    lane = lax.broadcasted_iota(jnp.int32, (HG_DK, DEC_BATCH), 1)

    def body(j, carry):
        bidx = i * HG_STEP_NB + j
        m = lane == bidx
        every = slice(0, D_MODEL)
        v_row, q_row, og_row = (_get_row(r, bidx, every) for r in (v_ref, q_s, og_ref))
        outs = []
        for h in range(HG_HEADS):
            hs = slice(h * HG_DK, (h + 1) * HG_DK)
            d_col = _column(dec_t[hs, :], m)
            kv = _dot(k_t[hs, :], _one_hot_rows(v_row[:, hs], bidx))
            s_new = d_col * s_ref[0, j, h] + kv
            so_ref[0, j, h] = s_new
            o_row = _row_times(q_row[:, hs], mat=s_new.astype(BF16))
            outs.append(_rms(o_row, nw_ref[0]) * _silu(og_row[:, hs]))
        _set_row(o_ref, bidx, every, jnp.concatenate(outs, axis=1))
        return carry

    lax.fori_loop(0, HG_STEP_NB, body, 0, unroll=4)


def _stacked_state_specs(state, nb, l, out_state):
    tail = state.shape[2:]
    zeros = (0,) * len(tail)
    in_spec = pl.BlockSpec((1, nb) + tail, lambda i: (l, i) + zeros)
    if out_state is None:
        return in_spec, pl.BlockSpec((DEPTH, nb) + tail, lambda i: (0, i) + zeros), [], [], {}
    out_spec = pl.BlockSpec((1, nb) + tail, lambda i: (l, i) + zeros)
    return in_spec, out_spec, [pl.BlockSpec(memory_space=pl.ANY)], [out_state], None


def _hgrn_step(proj, state, lbs, hg_norm, out_state, l):
    nb = HG_STEP_NB
    blk = lambda c0: pl.BlockSpec((DEC_BATCH, D_MODEL), lambda i: (0, c0 // D_MODEL))
    sspec, ospec, extra_specs, extra_args, aliases = _stacked_state_specs(state, nb, l, out_state)
    in_specs = [
        blk(COL_Q), blk(COL_F), blk(COL_I), blk(COL_OG),
        pl.BlockSpec((1, 1, D_MODEL), lambda i: (l, 0, 0)),
        pl.BlockSpec((1, 1, HG_DV), lambda i: (l, 0, 0)),
        sspec,
    ] + extra_specs
    args = [proj, proj, proj, proj, lbs, hg_norm, state] + extra_args
    if aliases is None:
        aliases = {len(args) - 1: 1}

    def body(*refs):
        ins, rest = refs[:7], refs[len(args):]
        _hgrn_step_kernel(*ins, *rest, first=(l == 0), zero_rest=(out_state is None))

    return pl.pallas_call(
        body,
        grid=(DEC_BATCH // nb,),
        in_specs=in_specs,
        out_specs=[pl.BlockSpec((DEC_BATCH, D_MODEL), lambda i: (0, 0)), ospec],
        out_shape=[
            jax.ShapeDtypeStruct((DEC_BATCH, D_MODEL), F32),
            jax.ShapeDtypeStruct(state.shape, F32),
        ],
        scratch_shapes=[pltpu.VMEM((D_MODEL, DEC_BATCH), F32), pltpu.VMEM((D_MODEL, DEC_BATCH), BF16),
                        pltpu.VMEM((DEC_BATCH, D_MODEL), F32)],
        input_output_aliases=aliases,
        compiler_params=_cparams("arbitrary"),
        name="hgrn_step",
    )(*args)


def _ssm_prep_kernel(xs_ref, bm_ref, cm_ref, dt_ref, cst_ref, w_ref, cb_ref,
                     alog_ref, exp_ref,
                     xc_ref, bc_ref, cc_ref, dec_ref, xdtt_ref, cnew_ref):
    ch = SSM_CONV_CH
    parts = ((xs_ref, 0, SSM_INNER, xc_ref),
             (bm_ref, SSM_INNER, SSM_INNER + 1024, bc_ref),
             (cm_ref, SSM_INNER + 1024, ch, cc_ref))
    for u_ref, c0, c1, out_ref in parts:
        u = u_ref[...]
        acc = cb_ref[:, c0:c1] + w_ref[SSM_CONV - 1:SSM_CONV, c0:c1] * u
        for j in range(SSM_CONV - 1):
            acc = acc + w_ref[j:j + 1, c0:c1] * cst_ref[:, j * ch + c0:j * ch + c1]
        out_ref[...] = _silu(acc)
        cnew_ref[:, (SSM_CONV - 2) * ch + c0:(SSM_CONV - 2) * ch + c1] = u
    cnew_ref[:, 0:(SSM_CONV - 2) * ch] = cst_ref[:, ch:(SSM_CONV - 1) * ch]
    dtv = _dot_sel(dt_ref[...], exp_ref[...])
    dec_ref[...] = jnp.exp(dtv * -jnp.exp(alog_ref[...]))
    xdt = xc_ref[...] * dtv
    for m in range(SSM_INNER // LANE):
        ms = slice(m * LANE, (m + 1) * LANE)
        xdtt_ref[ms, :] = xdt[:, ms].T.astype(BF16)


def _ssm_prep(proj, dt, conv_state, p, l):
    n = DEC_BATCH
    full = lambda shape: pl.BlockSpec(shape, lambda i: tuple(0 for _ in shape))
    lspec = lambda rows, width: pl.BlockSpec((1, rows, width), lambda i: (l, 0, 0))

    def body(xs, bm, cm, dtr, cst, w, cb, alog, ex, *outs):
        _ssm_prep_kernel(xs, bm, cm, dtr, cst.at[0], w.at[0], cb.at[0], alog.at[0], ex, *outs)

    return pl.pallas_call(
        body,
        grid=(1,),
        in_specs=[
            pl.BlockSpec((n, SSM_INNER), lambda i: (0, COL_XBC // SSM_INNER)),
            pl.BlockSpec((n, 1024), lambda i: (0, (COL_XBC + SSM_INNER) // 1024)),
            pl.BlockSpec((n, 1024), lambda i: (0, (COL_XBC + SSM_INNER) // 1024 + 1)),
            full((n, DT_PAD)),
            pl.BlockSpec((1, n, (SSM_CONV - 1) * SSM_CONV_CH), lambda i: (l, 0, 0)),
            lspec(SSM_CONV, SSM_CONV_CH), lspec(1, SSM_CONV_CH),
            lspec(1, SSM_INNER),
            full((DT_PAD, SSM_INNER)),
        ],
        out_specs=[
            full((n, SSM_INNER)), full((n, 1024)), full((n, 1024)),
            full((n, SSM_INNER)), full((SSM_INNER, n)),
            full((n, (SSM_CONV - 1) * SSM_CONV_CH)),
        ],
        out_shape=[
            jax.ShapeDtypeStruct((n, SSM_INNER), F32),
            jax.ShapeDtypeStruct((n, 1024), F32),
            jax.ShapeDtypeStruct((n, 1024), F32),
            jax.ShapeDtypeStruct((n, SSM_INNER), F32),
            jax.ShapeDtypeStruct((SSM_INNER, n), BF16),
            jax.ShapeDtypeStruct((n, (SSM_CONV - 1) * SSM_CONV_CH), F32),
        ],
        compiler_params=_cparams("arbitrary"),
        name="ssm_prep",
    )(proj, proj, proj, dt, conv_state, p['conv_w'], p['conv_b'], p['a_log_x'], p['expand_all'])


SSM_STEP_NB = 4


def _ssm_step_kernel(dec_ref, xdtt_ref, bc_ref, cc_ref, s_ref, y_ref, so_ref, *, zero_rest):
    i = pl.program_id(0)

    @pl.when(i == 0)
    def _():
        y_ref[...] = jnp.zeros_like(y_ref)

    if zero_rest:
        so_ref[1:] = jnp.zeros((DEPTH - 1,) + so_ref.shape[1:], F32)

    def body(j, carry):
        bidx = i * SSM_STEP_NB + j
        dec_row = _get_row(dec_ref, bidx, slice(0, SSM_INNER))
        b_row = _get_row(bc_ref, bidx, slice(0, SSM_GROUPS * SSM_STATE))
        c_row = _get_row(cc_ref, bidx, slice(0, SSM_GROUPS * SSM_STATE))
        ys = []
        for g in range(SSM_GROUPS):
            rs = slice(g * GROUP_W, (g + 1) * GROUP_W)
            hs = slice(g * SSM_HPG, (g + 1) * SSM_HPG)
            ns = slice(g * SSM_STATE, (g + 1) * SSM_STATE)
            decay = jnp.concatenate(
                [jnp.broadcast_to(dec_row[:, (g * SSM_HPG + k) * SSM_HEADDIM:
                                          (g * SSM_HPG + k) * SSM_HEADDIM + 1],
                                  (SSM_HEADDIM, SSM_STATE)) for k in range(SSM_HPG)], axis=0)
            inc = _dot(xdtt_ref[rs, :], _one_hot_rows(b_row[:, ns], bidx))
            h = s_ref[0, j, hs].reshape(GROUP_W, SSM_STATE)
            h_new = decay * h + inc
            so_ref[0, j, hs] = h_new.reshape(SSM_HPG, SSM_HEADDIM, SSM_STATE)
            ys.append(_row_times(c_row[:, ns], mat_nt=h_new.astype(BF16)))
        _set_row(y_ref, bidx, slice(0, SSM_INNER), jnp.concatenate(ys, axis=1))
        return carry

    lax.fori_loop(0, SSM_STEP_NB, body, 0, unroll=4)


def _ssm_step(dec, xdtt, bc, cc, state, out_state, l):
    nb = SSM_STEP_NB
    full = lambda shape: pl.BlockSpec(shape, lambda i: tuple(0 for _ in shape))
    sspec, ospec, extra_specs, extra_args, aliases = _stacked_state_specs(state, nb, l, out_state)
    in_specs = [full(dec.shape), full(xdtt.shape), full(bc.shape), full(cc.shape),
                sspec] + extra_specs
    args = [dec, xdtt, bc, cc, state] + extra_args
    if aliases is None:
        aliases = {len(args) - 1: 1}

    def body(*refs):
        ins, rest = refs[:5], refs[len(args):]
        _ssm_step_kernel(*ins, *rest, zero_rest=(out_state is None))

    return pl.pallas_call(
        body,
        grid=(DEC_BATCH // nb,),
        in_specs=in_specs,
        out_specs=[full((DEC_BATCH, SSM_INNER)), ospec],
        out_shape=[
            jax.ShapeDtypeStruct((DEC_BATCH, SSM_INNER), F32),
            jax.ShapeDtypeStruct(state.shape, F32),
        ],
        input_output_aliases=aliases,
        compiler_params=_cparams("arbitrary"),
        name="ssm_step",
    )(*args)


def _ssm_post_kernel(y_in_ref, xc_ref, z_ref, dsk_ref, nw_ref, y_ref):
    for g in range(SSM_GROUPS):
        gs = slice(g * GROUP_W, (g + 1) * GROUP_W)
        y = (y_in_ref[:, gs] + dsk_ref[:, gs] * xc_ref[:, gs]) * _silu(z_ref[:, gs])
        y_ref[:, gs] = _rms(y, nw_ref[:, gs])


def _ssm_post(y, xc, proj, p, l):
    n = DEC_BATCH
    full = lambda shape: pl.BlockSpec(shape, lambda i: tuple(0 for _ in shape))
    lspec = pl.BlockSpec((1, 1, SSM_INNER), lambda i: (l, 0, 0))

    def body(y_in, xc_r, z_r, dsk, nw, y_r):
        _ssm_post_kernel(y_in, xc_r, z_r, dsk.at[0], nw.at[0], y_r)

    return pl.pallas_call(
        body,
        grid=(1,),
        in_specs=[full((n, SSM_INNER)), full((n, SSM_INNER)),
                  pl.BlockSpec((n, SSM_INNER), lambda i: (0, COL_Z // SSM_INNER)),
                  lspec, lspec],
        out_specs=full((n, SSM_INNER)),
        out_shape=jax.ShapeDtypeStruct((n, SSM_INNER), F32),
        compiler_params=_cparams("arbitrary"),
        name="ssm_post",
    )(y, xc, proj, p['d_skip_x'], p['ssm_norm'])


def _merge_kernel(x_ref, oh_ref, y_ref, ga_ref, gb_ref, g1_ref, bma_ref, bmb_ref,
                  wa_ref, wb_ref, wo_ref, o_ref):
    a = _dot(oh_ref[...].astype(BF16), wa_ref[0])
    b = _dot(y_ref[...].astype(BF16), wb_ref[0])
    u = _sigmoid(ga_ref[...] + bma_ref[0]) * a + _sigmoid(gb_ref[...] + bmb_ref[0]) * b
    o_ref[...] = x_ref[...] + g1_ref[0] * _dot(u.astype(BF16), wo_ref[0])


def _merge(x, o_hg, y, proj, mod, p, l, tm, rows_per_group):
    rows = x.shape[0]
    wspec = lambda k: pl.BlockSpec((1, k, D_MODEL), lambda i: (l, 0, 0))
    return pl.pallas_call(
        _merge_kernel,
        grid=(rows // tm,),
        in_specs=[
            pl.BlockSpec((tm, D_MODEL), lambda i: (i, 0)),
            pl.BlockSpec((tm, D_MODEL), lambda i: (i, 0)),
            pl.BlockSpec((tm, SSM_INNER), lambda i: (i, 0)),
            pl.BlockSpec((tm, D_MODEL), lambda i: (i, COL_GA // D_MODEL)),
            pl.BlockSpec((tm, D_MODEL), lambda i: (i, COL_GB // D_MODEL)),
            _mod_spec(mod, tm, rows_per_group, ADA_G1),
            pl.BlockSpec((1, 1, D_MODEL), lambda i: (l, 0, 0)),
            pl.BlockSpec((1, 1, D_MODEL), lambda i: (l, 0, 1)),
            wspec(D_MODEL), wspec(SSM_INNER), wspec(D_MODEL),
        ],
        out_specs=pl.BlockSpec((tm, D_MODEL), lambda i: (i, 0)),
        out_shape=jax.ShapeDtypeStruct((rows, D_MODEL), F32),
        compiler_params=_cparams("arbitrary"),
        name="merge",
    )(x, o_hg, y, proj, proj, mod, p['b_merge'], p['b_merge'], p['w_br_a'], p['w_br_b'],
      p['w_out'])


FF_CHUNK = 1024


def _mlp_kernel(x_ref, sc_ref, sh_ref, g_ref, nw_ref, wu_ref, wd_ref, fw_ref, o_ref, *, final):
    x = x_ref[...]
    h = (_rms(x, nw_ref[0]) * (1.0 + sc_ref[0]) + sh_ref[0]).astype(BF16)
    acc = jnp.zeros(x.shape, F32)
    for c in range(D_FF // FF_CHUNK):
        cs = slice(c * FF_CHUNK, (c + 1) * FF_CHUNK)
        a = jnp.maximum(_dot(h, wu_ref[0, :, cs]), 0.0)
        acc = acc + _dot((a * a).astype(BF16), wd_ref[0, cs, :])
    out = x + g_ref[0] * acc
    if final:
        out = _rms(out, fw_ref[...])
    o_ref[...] = out


def _mlp(x, mod, p, norm_final, l, tm, rows_per_group):
    rows = x.shape[0]
    return pl.pallas_call(
        functools.partial(_mlp_kernel, final=(l == DEPTH - 1)),
        grid=(rows // tm,),
        in_specs=[
            pl.BlockSpec((tm, D_MODEL), lambda i: (i, 0)),
            _mod_spec(mod, tm, rows_per_group, ADA_SC2),
            _mod_spec(mod, tm, rows_per_group, ADA_SH2),
            _mod_spec(mod, tm, rows_per_group, ADA_G2),
            pl.BlockSpec((1, 1, D_MODEL), lambda i: (l, 0, 0)),
            pl.BlockSpec((1, D_MODEL, D_FF), lambda i: (l, 0, 0)),
            pl.BlockSpec((1, D_FF, D_MODEL), lambda i: (l, 0, 0)),
            pl.BlockSpec((1, D_MODEL), lambda i: (0, 0)),
        ],
        out_specs=pl.BlockSpec((tm, D_MODEL), lambda i: (i, 0)),
        out_shape=jax.ShapeDtypeStruct((rows, D_MODEL), F32),
        compiler_params=_cparams("arbitrary"),
        name="mlp",
    )(x, mod, mod, mod, p['norm_mlp'], p['w_up'], p['w_down'], norm_final)


def _prepare_params(w_ada, b_ada, norm_mix, w_in, b_merge, hg_norm, conv_w, conv_b, dt_bias,
                    a_log, d_skip, ssm_norm, w_br_a, w_br_b, w_out, norm_mlp, w_up, w_down):
    dt0 = COL_XBC + SSM_CONV_CH
    w_main = jnp.concatenate([w_in[:, :, :dt0], w_in[:, :, dt0 + SSM_HEADS:]], axis=-1)
    w_dt = jnp.pad(w_in[:, :, dt0:dt0 + SSM_HEADS], ((0, 0), (0, 0), (0, DT_PAD - SSM_HEADS)))
    per_head = lambda a: jnp.repeat(a, SSM_HEADDIM, axis=-1).reshape(DEPTH, 1, SSM_INNER)
    narrow = lambda a: jnp.pad(a, ((0, 0), (0, DT_PAD - SSM_HEADS))).reshape(DEPTH, 1, DT_PAD)
    head_of_col = jnp.arange(SSM_INNER) // SSM_HEADDIM
    expand_all = (jnp.arange(DT_PAD)[:, None] == head_of_col[None, :]).astype(BF16)
    expand_g = expand_all.reshape(DT_PAD, SSM_GROUPS, GROUP_W).transpose(1, 0, 2)
    return {
        'norm_mix': norm_mix.reshape(DEPTH, 1, D_MODEL),
        'w_main': w_main.astype(BF16), 'w_dt': w_dt.astype(BF16),
        'b_merge': b_merge.reshape(DEPTH, 1, 2 * D_MODEL),
        'hg_norm': hg_norm.reshape(DEPTH, 1, HG_DV),
        'conv_w': conv_w, 'conv_b': conv_b.reshape(DEPTH, 1, SSM_CONV_CH),
        'dt_bias_n': narrow(dt_bias), 'a_log_n': narrow(a_log),
        'a_log_x': per_head(a_log), 'd_skip_x': per_head(d_skip),
        'ssm_norm': ssm_norm.reshape(DEPTH, 1, SSM_INNER),
        'expand_all': expand_all, 'expand_g': expand_g,
        'w_br_a': w_br_a.astype(BF16), 'w_br_b': w_br_b.astype(BF16),
        'w_out': w_out.astype(BF16),
        'norm_mlp': norm_mlp.reshape(DEPTH, 1, D_MODEL),
        'w_up': w_up.astype(BF16), 'w_down': w_down.astype(BF16),
    }


def kernel(x_prompt, x_sample, state_hgrn, state_ssm, state_conv, c_prompt, c_sample, w_ada, b_ada, norm_mix, w_in, b_merge, lower_bounds, hg_norm, conv_w, conv_b, dt_bias, a_log, d_skip, ssm_norm, w_br_a, w_br_b, w_out, norm_mlp, w_up, w_down, norm_final):
    p = _prepare_params(w_ada, b_ada, norm_mix, w_in, b_merge, hg_norm, conv_w, conv_b, dt_bias,
                        a_log, d_skip, ssm_norm, w_br_a, w_br_b, w_out, norm_mlp, w_up, w_down)
    lbs = _lower_bounds(lower_bounds).reshape(DEPTH, 1, HG_HEADS * HG_DK)
    mod = _modulation(jnp.concatenate([c_prompt, c_sample], axis=0), w_ada, b_ada)
    mod_p = mod[:, :BATCH].reshape(DEPTH, BATCH, 1, N_ADA * D_MODEL)
    mod_s = mod[:, BATCH:].reshape(DEPTH, 1, DEC_BATCH, N_ADA * D_MODEL)
    norm_final2 = norm_final.reshape(1, D_MODEL)
    conv_state = state_conv.reshape(DEPTH, DEC_BATCH, (SSM_CONV - 1) * SSM_CONV_CH)

    xp = x_prompt.reshape(BATCH * SEQ, D_MODEL)
    xs = x_sample.reshape(DEC_BATCH, D_MODEL)
    hg_p, ssm_p, cv_p, cv_s = [], [], [], []
    hg_s = ssm_s = None
    for l in range(DEPTH):
        proj, dt, cum = _inproj(xp, mod_p[l], p, l, SEQ, SEQ, 1)
        o_hg, s_new, y, h_new = _mixer_prompt(proj, dt, cum, lbs, p, l)
        hg_p.append(s_new)
        ssm_p.append(h_new)
        cv_p.append(proj.reshape(BATCH, SEQ, PROJ_W)[:, SEQ - (SSM_CONV - 1):,
                                                       COL_XBC:COL_XBC + SSM_CONV_CH])
        xp = _merge(xp, o_hg, y, proj, mod_p[l], p, l, 512, SEQ)
        xp = _mlp(xp, mod_p[l], p, norm_final2, l, 512, SEQ)
        proj, dt, _ = _inproj(xs, mod_s[l], p, l, DEC_BATCH, DEC_BATCH, 3)
        o_hg, hg_s = _hgrn_step(proj, state_hgrn, lbs, p['hg_norm'], hg_s, l)
        xc, bc, cc, dect, xdtt, cnew = _ssm_prep(proj, dt, conv_state, p, l)
        yt, ssm_s = _ssm_step(dect, xdtt, bc, cc, state_ssm, ssm_s, l)
        y = _ssm_post(yt, xc, proj, p, l)
        cv_s.append(cnew.reshape(DEC_BATCH, SSM_CONV - 1, SSM_CONV_CH))
        xs = _merge(xs, o_hg, y, proj, mod_s[l], p, l, DEC_BATCH, DEC_BATCH)
        xs = _mlp(xs, mod_s[l], p, norm_final2, l, DEC_BATCH, DEC_BATCH)
    return (xp.reshape(BATCH, SEQ, D_MODEL), xs.reshape(DEC_BATCH, 1, D_MODEL),
            jnp.stack(hg_p), jnp.stack(ssm_p), jnp.stack(cv_p),
            hg_s, ssm_s, jnp.stack(cv_s))
```
